```python
import jax, jax.numpy as jnp
from jax import lax
import numpy as np

D_MODEL = 2048
BATCH = 2
SEQ = 8192
DEPTH = 4

HEAD_DIM = 64
N_Q_HEADS = 16
N_KV_HEADS = 4
ATTN_WIDTH = N_Q_HEADS * HEAD_DIM
KV_WIDTH = N_KV_HEADS * HEAD_DIM
WINDOW = 128
ATTN_BLOCK = 128
ROPE_THETA = 10000.0

GLA_HEADS = 4
GLA_V_WIDTH = D_MODEL - ATTN_WIDTH
GLA_K_WIDTH = GLA_V_WIDTH // 2
GLA_DK = GLA_K_WIDTH // GLA_HEADS
GLA_DV = GLA_V_WIDTH // GLA_HEADS
GLA_GATE_RANK = 16
GLA_TAU = 16.0
GLA_CHUNK = 64

MIX_SPLITS = (ATTN_WIDTH, KV_WIDTH, KV_WIDTH, GLA_K_WIDTH, GLA_K_WIDTH, GLA_V_WIDTH, GLA_V_WIDTH, GLA_GATE_RANK)
MIX_IN_WIDTH = ATTN_WIDTH + 2 * KV_WIDTH + 2 * GLA_K_WIDTH + 2 * GLA_V_WIDTH + GLA_GATE_RANK

D_FF = 5504
NORM_EPS = 1e-6
N_SANDWICH_NORMS = 6

kernel_name = 'hybrid_swa_sink_gla_macaron_sandwich'


def rmsnorm(x, gain):
    xf = x.astype(jnp.float32)
    y = xf * lax.rsqrt(jnp.mean(xf * xf, axis=-1, keepdims=True) + NORM_EPS)
    return (y * gain.astype(jnp.float32)).astype(x.dtype)


def rope(x, positions):
    d = x.shape[-1]
    inv_freq = ROPE_THETA ** (-jnp.arange(0, d, 2, dtype=jnp.float32) / d)
    ang = positions.astype(jnp.float32)[..., None] * inv_freq
    cos = jnp.cos(ang)[:, :, None, :]
    sin = jnp.sin(ang)[:, :, None, :]
    xf = x.astype(jnp.float32)
    x1, x2 = xf[..., : d // 2], xf[..., d // 2:]
    return jnp.concatenate([x1 * cos - x2 * sin, x2 * cos + x1 * sin], axis=-1).astype(x.dtype)


def sliding_window_attention(q, k, v, sinks):
    B, S = q.shape[0], q.shape[1]
    nb = S // ATTN_BLOCK
    G = N_Q_HEADS // N_KV_HEADS
    qb = q.reshape(B, nb, ATTN_BLOCK, N_KV_HEADS, G, HEAD_DIM)

    def with_prev(t):
        tb = t.reshape(B, nb, ATTN_BLOCK, N_KV_HEADS, HEAD_DIM)
        prev = jnp.pad(tb[:, :-1], ((0, 0), (1, 0), (0, 0), (0, 0), (0, 0)))
        return jnp.concatenate([prev, tb], axis=2)

    kb, vb = with_prev(k), with_prev(v)
    scores = jnp.einsum('bnqhgd,bnkhd->bnhgqk', qb, kb,
                        preferred_element_type=jnp.float32) * (HEAD_DIM ** -0.5)
    qi = jnp.arange(ATTN_BLOCK)[:, None]
    kj = jnp.arange(2 * ATTN_BLOCK)[None, :]
    diff = qi + ATTN_BLOCK - kj
    band = (diff >= 0) & (diff < WINDOW)
    key_valid = (jnp.arange(nb)[:, None] > 0) | (kj >= ATTN_BLOCK)
    mask = band[None] & key_valid[:, None, :]
    scores = jnp.where(mask[None, :, None, None], scores, -jnp.inf)
    sink = sinks.astype(jnp.float32).reshape(N_KV_HEADS, G)[None, None, :, :, None]
    m = jnp.maximum(scores.max(axis=-1), sink)
    p = jnp.exp(scores - m[..., None])
    denom = p.sum(axis=-1) + jnp.exp(sink - m)
    p = p / denom[..., None]
    out = jnp.einsum('bnhgqk,bnkhd->bnqhgd', p, vb.astype(jnp.float32))
    return out.reshape(B, S, ATTN_WIDTH).astype(q.dtype)


def gated_linear_attention(q, k, v, log_a):
    B, S = q.shape[0], q.shape[1]
    N, C = S // GLA_CHUNK, GLA_CHUNK
    q = q.astype(jnp.float32).reshape(B, N, C, GLA_HEADS, GLA_DK) * (GLA_DK ** -0.5)
    k = k.astype(jnp.float32).reshape(B, N, C, GLA_HEADS, GLA_DK)
    v = v.astype(jnp.float32).reshape(B, N, C, GLA_HEADS, GLA_DV)
    b = jnp.cumsum(log_a.astype(jnp.float32).reshape(B, N, C, GLA_HEADS, GLA_DK), axis=2)
    b_last = b[:, :, -1:]
    q_dec = q * jnp.exp(b)
    k_dec = k * jnp.exp(-b)
    causal = jnp.tril(jnp.ones((C, C), dtype=bool))
    intra = jnp.einsum('bnthk,bnshk->bnhts', q_dec, k_dec)
    intra = jnp.where(causal, intra, 0.0)
    o_intra = jnp.einsum('bnhts,bnshv->bnthv', intra, v)
    dS = jnp.einsum('bnshk,bnshv->bnhkv', k * jnp.exp(b_last - b), v)
    chunk_decay = jnp.exp(b_last[:, :, 0])

    def step(state, inp):
        dec, ds = inp
        return dec[..., None] * state + ds, state

    init = jnp.zeros((B, GLA_HEADS, GLA_DK, GLA_DV), jnp.float32)
    _, s_before = lax.scan(step, init, (jnp.moveaxis(chunk_decay, 1, 0), jnp.moveaxis(dS, 1, 0)))
    s_before = jnp.moveaxis(s_before, 0, 1)
    o_inter = jnp.einsum('bnthk,bnhkv->bnthv', q_dec, s_before)
    return (o_intra + o_inter).reshape(B, S, GLA_HEADS, GLA_DV)


def hybrid_mixer(h, positions, w_in, sinks, gate_w2, gate_b, gla_norm_gain, w_out):
    B, S = h.shape[0], h.shape[1]
    split_at = tuple(int(i) for i in np.cumsum(MIX_SPLITS)[:-1])
    q_a, k_a, v_a, q_g, k_g, v_g, r_g, g_lr = jnp.split(h @ w_in, split_at, axis=-1)
    q_a = rope(q_a.reshape(B, S, N_Q_HEADS, HEAD_DIM), positions)
    k_a = rope(k_a.reshape(B, S, N_KV_HEADS, HEAD_DIM), positions)
    v_a = v_a.reshape(B, S, N_KV_HEADS, HEAD_DIM)
    attn_out = sliding_window_attention(q_a, k_a, v_a, sinks)
    log_a = jax.nn.log_sigmoid((g_lr @ gate_w2 + gate_b).astype(jnp.float32)) / GLA_TAU
    o = gated_linear_attention(q_g.reshape(B, S, GLA_HEADS, GLA_DK),
                               k_g.reshape(B, S, GLA_HEADS, GLA_DK),
                               v_g.reshape(B, S, GLA_HEADS, GLA_DV),
                               log_a.reshape(B, S, GLA_HEADS, GLA_DK))
    o = rmsnorm(o, gla_norm_gain).reshape(B, S, GLA_V_WIDTH)
    gla_out = (o * jax.nn.silu(r_g.astype(jnp.float32))).astype(h.dtype)
    return jnp.concatenate([attn_out.astype(h.dtype), gla_out], axis=-1) @ w_out


def swiglu(h, w_in, w_out):
    gate, up = jnp.split(h @ w_in, 2, axis=-1)
    return (jax.nn.silu(gate) * up) @ w_out


def setup_inputs(seed: int = 0) -> dict:
    key = jax.random.key(seed)
    ks = jax.random.split(key, 11)
    f32 = jnp.float32
    x = jax.random.normal(ks[0], (BATCH, SEQ, D_MODEL), f32)
    positions = jnp.broadcast_to(jnp.arange(SEQ, dtype=jnp.int32), (BATCH, SEQ))
    norm_gains = 1.0 + 0.05 * jax.random.normal(ks[1], (DEPTH, N_SANDWICH_NORMS, D_MODEL), f32)
    ffn_w_in = jax.random.normal(ks[2], (DEPTH, 2, D_MODEL, 2 * D_FF), f32) * D_MODEL ** -0.5
    ffn_w_out = jax.random.normal(ks[3], (DEPTH, 2, D_FF, D_MODEL), f32) * D_FF ** -0.5
    w_mix_in = jax.random.normal(ks[4], (DEPTH, D_MODEL, MIX_IN_WIDTH), f32) * D_MODEL ** -0.5
    attn_sinks = jax.random.normal(ks[5], (DEPTH, N_Q_HEADS), f32)
    gla_gate_w2 = jax.random.normal(ks[6], (DEPTH, GLA_GATE_RANK, GLA_K_WIDTH), f32) * GLA_GATE_RANK ** -0.5
    gla_gate_b = 0.1 * jax.random.normal(ks[7], (DEPTH, GLA_K_WIDTH), f32)
    gla_norm_gain = 1.0 + 0.05 * jax.random.normal(ks[8], (DEPTH, GLA_DV), f32)
    w_mix_out = jax.random.normal(ks[9], (DEPTH, D_MODEL, D_MODEL), f32) * D_MODEL ** -0.5
    return {'x': x, 'positions': positions, 'norm_gains': norm_gains, 'ffn_w_in': ffn_w_in,
            'ffn_w_out': ffn_w_out, 'w_mix_in': w_mix_in, 'attn_sinks': attn_sinks,
            'gla_gate_w2': gla_gate_w2, 'gla_gate_b': gla_gate_b, 'gla_norm_gain': gla_norm_gain,
            'w_mix_out': w_mix_out}


def reference(x, positions, norm_gains, ffn_w_in, ffn_w_out, w_mix_in, attn_sinks,
              gla_gate_w2, gla_gate_b, gla_norm_gain, w_mix_out):
    for l in range(DEPTH):
        g = norm_gains[l]
        h = swiglu(rmsnorm(x, g[0]), ffn_w_in[l, 0], ffn_w_out[l, 0])
        x = x + 0.5 * rmsnorm(h, g[1])
        h = hybrid_mixer(rmsnorm(x, g[2]), positions, w_mix_in[l], attn_sinks[l],
                         gla_gate_w2[l], gla_gate_b[l], gla_norm_gain[l], w_mix_out[l])
        x = x + rmsnorm(h, g[3])
        h = swiglu(rmsnorm(x, g[4]), ffn_w_in[l, 1], ffn_w_out[l, 1])
        x = x + 0.5 * rmsnorm(h, g[5])
    return x
```

```python
import functools

import numpy as np
import jax
import jax.numpy as jnp
from jax import lax
from jax.experimental import pallas as pl
from jax.experimental.pallas import tpu as pltpu

F32 = jnp.float32
BF16 = jnp.bfloat16

HEAD_DIM = 64
N_Q_HEADS = 16
N_KV_HEADS = 4
Q_PER_KV = N_Q_HEADS // N_KV_HEADS
ATTN_WIDTH = N_Q_HEADS * HEAD_DIM
KV_WIDTH = N_KV_HEADS * HEAD_DIM
ATTN_BLOCK = 128
ROPE_THETA = 10000.0
GLA_HEADS = 4
GLA_DK = 128
GLA_DV = 256
GLA_K_WIDTH = GLA_HEADS * GLA_DK
GLA_V_WIDTH = GLA_HEADS * GLA_DV
GLA_GATE_RANK = 16
GLA_TAU = 16.0
GLA_CHUNK = 64
NORM_EPS = 1e-6

LANES = 128
VMEM_LIMIT_BYTES = 60000 * 1024

FFN_TM = 512
FFN_TF = 512
PROJ_TM = 512
ATTN_TQ = 256
GLA_TS = 256


def _params(*sem):
    return pltpu.CompilerParams(dimension_semantics=sem, vmem_limit_bytes=VMEM_LIMIT_BYTES)


def _resident(shape):
    return pl.BlockSpec(shape, lambda *_: (0,) * len(shape), pipeline_mode=pl.Buffered(1))


def _rms(x, gain):
    ms = jnp.mean(x * x, axis=-1, keepdims=True)
    return (x * lax.rsqrt(ms + NORM_EPS)) * gain


def _silu(x):
    return x * jax.nn.sigmoid(x)


def _ffn_kernel(x_ref, gin_ref, gout_ref, wg_ref, wu_ref, wo_ref, o_ref, hn_ref):
    j = pl.program_id(1)

    @pl.when(j == 0)
    def _():
        hn_ref[...] = _rms(x_ref[...], gin_ref[...]).astype(BF16)
        o_ref[...] = jnp.zeros_like(o_ref)

    hn = hn_ref[...]
    gate = jnp.dot(hn, wg_ref[...], preferred_element_type=F32)
    up = jnp.dot(hn, wu_ref[...], preferred_element_type=F32)
    act = (_silu(gate) * up).astype(BF16)
    o_ref[...] += jnp.dot(act, wo_ref[...], preferred_element_type=F32)

    @pl.when(j == pl.num_programs(1) - 1)
    def _():
        o_ref[...] = x_ref[...] + 0.5 * _rms(o_ref[...], gout_ref[...])


def _ffn(x, gin, gout, w_in, w_out):
    t, d = x.shape
    fp = w_out.shape[0]
    nf = fp // FFN_TF
    return pl.pallas_call(
        _ffn_kernel,
        out_shape=jax.ShapeDtypeStruct((t, d), F32),
        grid=(t // FFN_TM, nf),
        in_specs=[
            pl.BlockSpec((FFN_TM, d), lambda i, j: (i, 0)),
            pl.BlockSpec((1, d), lambda i, j: (0, 0)),
            pl.BlockSpec((1, d), lambda i, j: (0, 0)),
            pl.BlockSpec((d, FFN_TF), lambda i, j: (0, j)),
            pl.BlockSpec((d, FFN_TF), lambda i, j: (0, nf + j)),
            pl.BlockSpec((FFN_TF, d), lambda i, j: (j, 0)),
        ],
        out_specs=pl.BlockSpec((FFN_TM, d), lambda i, j: (i, 0)),
        scratch_shapes=[pltpu.VMEM((FFN_TM, d), BF16)],
        compiler_params=_params("parallel", "arbitrary"),
        name="ffn",
    )(x, gin, gout, w_in, w_in, w_out)


def _rope_table_kernel(pos_ref, invf_ref, cos_ref, sin_ref):
    ang = pos_ref[...].astype(F32) * invf_ref[...]
    lane = lax.broadcasted_iota(jnp.int32, ang.shape, 1)
    first_half = (lane % HEAD_DIM) < (HEAD_DIM // 2)
    cos_ref[...] = jnp.cos(ang)
    sin_ref[...] = jnp.where(first_half, -jnp.sin(ang), jnp.sin(ang))


def _rope_tables(positions):
    t = positions.size
    tm = min(t, 2048)
    inv_freq = ROPE_THETA ** (-jnp.arange(0, HEAD_DIM, 2, dtype=F32) / HEAD_DIM)
    invf = jnp.tile(inv_freq, LANES // (HEAD_DIM // 2)).reshape(1, LANES)
    return pl.pallas_call(
        _rope_table_kernel,
        out_shape=[jax.ShapeDtypeStruct((t, LANES), F32)] * 2,
        grid=(t // tm,),
        in_specs=[pl.BlockSpec((tm, 1), lambda i: (i, 0)),
                  pl.BlockSpec((1, LANES), lambda i: (0, 0))],
        out_specs=[pl.BlockSpec((tm, LANES), lambda i: (i, 0))] * 2,
        compiler_params=_params("parallel"),
        name="rope_tables",
    )(positions.reshape(t, 1), invf)


def _rope(x, cos, sin, first_half):
    half = HEAD_DIM // 2
    partner = jnp.where(first_half, pltpu.roll(x, LANES - half, axis=1), pltpu.roll(x, half, axis=1))
    return x * cos + partner * sin


def _inproj_kernel(x_ref, g_ref, w_ref, wl_ref, w2_ref, gb_ref, cos_ref, sin_ref,
                   qa_ref, ka_ref, va_ref, qg_ref, kg_ref, vg_ref, rg_ref, la_ref):
    hn = _rms(x_ref[...], g_ref[...]).astype(BF16)
    cos = cos_ref[...]
    sin = sin_ref[...]
    lane = lax.broadcasted_iota(jnp.int32, cos.shape, 1)
    first_half = (lane % HEAD_DIM) < (HEAD_DIM // 2)

    def proj(c0, width):
        return jnp.dot(hn, w_ref[:, c0:c0 + width], preferred_element_type=F32)

    c = 0
    for s in range(ATTN_WIDTH // LANES):
        qa_ref[:, s * LANES:(s + 1) * LANES] = _rope(
            proj(c + s * LANES, LANES), cos, sin, first_half).astype(BF16)
    c += ATTN_WIDTH
    for s in range(KV_WIDTH // LANES):
        ka_ref[:, s * LANES:(s + 1) * LANES] = _rope(
            proj(c + s * LANES, LANES), cos, sin, first_half).astype(BF16)
    c += KV_WIDTH
    va_ref[...] = proj(c, KV_WIDTH).astype(BF16)
    c += KV_WIDTH
    qg_ref[...] = proj(c, GLA_K_WIDTH)
    c += GLA_K_WIDTH
    kg_ref[...] = proj(c, GLA_K_WIDTH)
    c += GLA_K_WIDTH
    vg_ref[...] = proj(c, GLA_V_WIDTH).astype(BF16)
    c += GLA_V_WIDTH
    rg_ref[...] = proj(c, GLA_V_WIDTH).astype(BF16)

    g_lr = jnp.dot(hn, wl_ref[...], preferred_element_type=F32).astype(BF16)
    z = jnp.dot(g_lr, w2_ref[...], preferred_element_type=F32) + gb_ref[...]
    la_ref[...] = jax.nn.log_sigmoid(z) / GLA_TAU


def _inproj(x, gain, w_main, w_lr, w2, gate_b, cos, sin):
    t, d = x.shape
    tm = PROJ_TM
    row = lambda width: pl.BlockSpec((tm, width), lambda i: (i, 0))
    widths = (ATTN_WIDTH, KV_WIDTH, KV_WIDTH, GLA_K_WIDTH, GLA_K_WIDTH, GLA_V_WIDTH, GLA_V_WIDTH,
              GLA_K_WIDTH)
    dtypes = (BF16, BF16, BF16, F32, F32, BF16, BF16, F32)
    return pl.pallas_call(
        _inproj_kernel,
        out_shape=[jax.ShapeDtypeStruct((t, w), dt) for w, dt in zip(widths, dtypes)],
        grid=(t // tm,),
        in_specs=[row(d), _resident((1, d)), _resident(w_main.shape), _resident(w_lr.shape),
                  _resident(w2.shape), _resident((1, GLA_K_WIDTH)), row(LANES), row(LANES)],
        out_specs=[row(w) for w in widths],
        compiler_params=_params("parallel"),
        name="mixer_inproj",
    )(x, gain, w_main, w_lr, w2, gate_b, cos, sin)


def _attn_kernel(sinks_ref, q_ref, kc_ref, vc_ref, kp_ref, vp_ref, o_ref):
    n = pl.program_id(1)
    blk = ATTN_BLOCK
    k_all = jnp.concatenate([kp_ref[...], kc_ref[...]], axis=0)
    v_all = jnp.concatenate([vp_ref[...], vc_ref[...]], axis=0)
    qi = lax.broadcasted_iota(jnp.int32, (blk, 2 * blk), 0)
    kj = lax.broadcasted_iota(jnp.int32, (blk, 2 * blk), 1)
    diff = qi + blk - kj
    band = (diff >= 0) & (diff < blk)
    first_valid = jnp.where(n > 0, 0, blk)
    scale = HEAD_DIM ** -0.5
    for b in range(q_ref.shape[0] // blk):
        rows = slice(b * blk, (b + 1) * blk)
        mask = band & (kj >= first_valid) if b == 0 else band
        outs = []
        for h in range(N_Q_HEADS):
            g = h // Q_PER_KV
            q = q_ref[rows, h * HEAD_DIM:(h + 1) * HEAD_DIM]
            k = k_all[b * blk:(b + 2) * blk, g * HEAD_DIM:(g + 1) * HEAD_DIM]
            v = v_all[b * blk:(b + 2) * blk, g * HEAD_DIM:(g + 1) * HEAD_DIM]
            s = lax.dot_general(q, k, (((1,), (1,)), ((), ())), preferred_element_type=F32) * scale
            s = jnp.where(mask, s, -jnp.inf)
            sink = sinks_ref[h]
            m = jnp.maximum(jnp.max(s, axis=-1, keepdims=True), sink)
            p = jnp.exp(s - m)
            denom = jnp.sum(p, axis=-1, keepdims=True) + jnp.exp(sink - m)
            pv = jnp.dot(p.astype(BF16), v, preferred_element_type=F32)
            outs.append(pv / denom)
        o_ref[rows, :] = jnp.concatenate(outs, axis=1).astype(BF16)


def _attention(q, k, v, sinks, batch):
    t = q.shape[0]
    seq = t // batch
    tq = min(ATTN_TQ, seq)
    nq = seq // tq
    per = tq // ATTN_BLOCK
    cur = lambda width: pl.BlockSpec((tq, width), lambda b, n: (b * nq + n, 0))
    prev = lambda width: pl.BlockSpec(
        (ATTN_BLOCK, width), lambda b, n: (jnp.maximum((b * nq + n) * per - 1, 0), 0))
    return pl.pallas_call(
        _attn_kernel,
        out_shape=jax.ShapeDtypeStruct((t, ATTN_WIDTH), BF16),
        grid=(batch, nq),
        in_specs=[pl.BlockSpec(memory_space=pltpu.SMEM),
                  cur(ATTN_WIDTH), cur(KV_WIDTH), cur(KV_WIDTH), prev(KV_WIDTH), prev(KV_WIDTH)],
        out_specs=cur(ATTN_WIDTH),
        compiler_params=_params("parallel", "arbitrary"),
        name="swa_attention",
    )(sinks, q, k, v, k, v)


def _gla_kernel(q_ref, k_ref, v_ref, la_ref, r_ref, gain_ref, o_ref, state_ref):
    @pl.when(pl.program_id(1) == 0)
    def _():
        state_ref[...] = jnp.zeros_like(state_ref)

    c = GLA_CHUNK
    ti = lax.broadcasted_iota(jnp.int32, (c, c), 0)
    si = lax.broadcasted_iota(jnp.int32, (c, c), 1)
    causal = ti >= si
    tri = causal.astype(F32)
    gain = gain_ref[...]
    scale = GLA_DK ** -0.5
    for ci in range(q_ref.shape[0] // c):
        rows = slice(ci * c, (ci + 1) * c)
        la = la_ref[rows, :]
        b = jnp.dot(tri, la, precision=lax.Precision.HIGHEST, preferred_element_type=F32)
        b_last = b[c - 1:c, :]
        q_dec = (q_ref[rows, :] * scale) * jnp.exp(b)
        k_all = k_ref[rows, :]
        k_dec = k_all * jnp.exp(-b)
        k_rem = k_all * jnp.exp(b_last - b)
        chunk_decay = jnp.exp(b_last)
        for h in range(GLA_HEADS):
            kl = slice(h * GLA_DK, (h + 1) * GLA_DK)
            vl = slice(h * GLA_DV, (h + 1) * GLA_DV)
            qh = q_dec[:, kl].astype(BF16)
            vh = v_ref[rows, vl]
            intra = lax.dot_general(qh, k_dec[:, kl].astype(BF16), (((1,), (1,)), ((), ())),
                                    preferred_element_type=F32)
            intra = jnp.where(causal, intra, 0.0).astype(BF16)
            state = state_ref[h]
            o = (jnp.dot(intra, vh, preferred_element_type=F32)
                 + jnp.dot(qh, state.astype(BF16), preferred_element_type=F32))
            d_state = jnp.dot(k_rem[:, kl].T.astype(BF16), vh, preferred_element_type=F32)
            decay_col = jnp.broadcast_to(chunk_decay[:, kl], (GLA_DK, GLA_DK)).T
            state_ref[h] = jnp.concatenate([decay_col] * (GLA_DV // GLA_DK), axis=1) * state + d_state
            r = r_ref[rows, vl].astype(F32)
            o_ref[rows, vl] = (_rms(o, gain) * _silu(r)).astype(BF16)


def _gla(q, k, v, log_a, r, gain, batch):
    t = q.shape[0]
    seq = t // batch
    ts = min(GLA_TS, seq)
    ns = seq // ts
    row = lambda width: pl.BlockSpec((ts, width), lambda b, n: (b * ns + n, 0))
    return pl.pallas_call(
        _gla_kernel,
        out_shape=jax.ShapeDtypeStruct((t, GLA_V_WIDTH), BF16),
        grid=(batch, ns),
        in_specs=[row(GLA_K_WIDTH), row(GLA_K_WIDTH), row(GLA_V_WIDTH), row(GLA_K_WIDTH),
                  row(GLA_V_WIDTH), pl.BlockSpec((1, GLA_DV), lambda b, n: (0, 0))],
        out_specs=row(GLA_V_WIDTH),
        scratch_shapes=[pltpu.VMEM((GLA_HEADS, GLA_DK, GLA_DV), F32)],
        compiler_params=_params("parallel", "arbitrary"),
        name="gla",
    )(q, k, v, log_a, r, gain)


def _outproj_kernel(x_ref, a_ref, g_ref, w_ref, gain_ref, o_ref):
    h = (jnp.dot(a_ref[...], w_ref[:ATTN_WIDTH, :], preferred_element_type=F32)
         + jnp.dot(g_ref[...], w_ref[ATTN_WIDTH:, :], preferred_element_type=F32))
    o_ref[...] = x_ref[...] + _rms(h, gain_ref[...])


def _outproj(x, attn, gla, w_out, gain):
    t, d = x.shape
    tm = PROJ_TM
    row = lambda width: pl.BlockSpec((tm, width), lambda i: (i, 0))
    return pl.pallas_call(
        _outproj_kernel,
        out_shape=jax.ShapeDtypeStruct((t, d), F32),
        grid=(t // tm,),
        in_specs=[row(d), row(ATTN_WIDTH), row(GLA_V_WIDTH), _resident(w_out.shape),
                  _resident((1, d))],
        out_specs=row(d),
        compiler_params=_params("parallel"),
        name="mixer_outproj",
    )(x, attn, gla, w_out, gain)


def _prep_ffn_weights(w_in, w_out):
    d_ff = w_out.shape[0]
    pad = (-d_ff) % FFN_TF
    gate, up = w_in[:, :d_ff], w_in[:, d_ff:]
    w_in_p = jnp.concatenate(
        [jnp.pad(gate, ((0, 0), (0, pad))), jnp.pad(up, ((0, 0), (0, pad)))], axis=1).astype(BF16)
    w_out_p = jnp.pad(w_out, ((0, pad), (0, 0))).astype(BF16)
    return w_in_p, w_out_p


def kernel(x, positions, norm_gains, ffn_w_in, ffn_w_out, w_mix_in, attn_sinks, gla_gate_w2,
           gla_gate_b, gla_norm_gain, w_mix_out):
    batch, seq, d = x.shape
    depth = norm_gains.shape[0]
    xt = x.reshape(batch * seq, d)
    cos, sin = _rope_tables(positions)
    main = w_mix_in.shape[-1] - GLA_GATE_RANK
    for l in range(depth):
        g = norm_gains[l].reshape(-1, 1, d)
        w_in0, w_out0 = _prep_ffn_weights(ffn_w_in[l, 0], ffn_w_out[l, 0])
        xt = _ffn(xt, g[0], g[1], w_in0, w_out0)

        w_main = w_mix_in[l, :, :main].astype(BF16)
        w_lr = jnp.pad(w_mix_in[l, :, main:], ((0, 0), (0, LANES - GLA_GATE_RANK))).astype(BF16)
        w2 = jnp.pad(gla_gate_w2[l], ((0, LANES - GLA_GATE_RANK), (0, 0))).astype(BF16)
        qa, ka, va, qg, kg, vg, rg, la = _inproj(
            xt, g[2], w_main, w_lr, w2, gla_gate_b[l].reshape(1, -1), cos, sin)
        attn = _attention(qa, ka, va, attn_sinks[l], batch)
        gla = _gla(qg, kg, vg, la, rg, gla_norm_gain[l].reshape(1, -1), batch)
        xt = _outproj(xt, attn, gla, w_mix_out[l].astype(BF16), g[3])

        w_in1, w_out1 = _prep_ffn_weights(ffn_w_in[l, 1], ffn_w_out[l, 1])
        xt = _ffn(xt, g[4], g[5], w_in1, w_out1)
    return xt.reshape(batch, seq, d)
```

```python
import functools

import numpy as np
import jax
import jax.numpy as jnp
from jax import lax
from jax.experimental import pallas as pl
from jax.experimental.pallas import tpu as pltpu

F32 = jnp.float32
BF16 = jnp.bfloat16

HEAD_DIM = 64
N_Q_HEADS = 16
N_KV_HEADS = 4
Q_PER_KV = N_Q_HEADS // N_KV_HEADS
ATTN_WIDTH = N_Q_HEADS * HEAD_DIM
KV_WIDTH = N_KV_HEADS * HEAD_DIM
ATTN_BLOCK = 128
ROPE_THETA = 10000.0
GLA_HEADS = 4
GLA_DK = 128
GLA_DV = 256
GLA_K_WIDTH = GLA_HEADS * GLA_DK
GLA_V_WIDTH = GLA_HEADS * GLA_DV
GLA_GATE_RANK = 16
GLA_TAU = 16.0
GLA_CHUNK = 64
NORM_EPS = 1e-6

LANES = 128
VMEM_LIMIT_BYTES = 60000 * 1024

FFN_TM = 512
FFN_TF = 512
PROJ_TM = 512
ATTN_TQ = 256
GLA_TS = 256


def _params(*sem):
    return pltpu.CompilerParams(dimension_semantics=sem, vmem_limit_bytes=VMEM_LIMIT_BYTES)


def _resident(shape, layer=None):
    if layer is None:
        return pl.BlockSpec(shape, lambda *_: (0,) * len(shape), pipeline_mode=pl.Buffered(1))
    return pl.BlockSpec((None,) + tuple(shape), lambda *_: (layer,) + (0,) * len(shape),
                        pipeline_mode=pl.Buffered(1))


def _rms(x, gain):
    ms = jnp.mean(x * x, axis=-1, keepdims=True)
    return (x * lax.rsqrt(ms + NORM_EPS)) * gain


def _silu(x):
    return x * jax.nn.sigmoid(x)


def _ffn_kernel(x_ref, gin_ref, gout_ref, wg_ref, wu_ref, wo_ref, o_ref, hn_ref):
    j = pl.program_id(1)

    @pl.when(j == 0)
    def _():
        hn_ref[...] = _rms(x_ref[...], gin_ref[...]).astype(BF16)
        o_ref[...] = jnp.zeros_like(o_ref)

    hn = hn_ref[...]
    gate = jnp.dot(hn, wg_ref[...], preferred_element_type=F32)
    up = jnp.dot(hn, wu_ref[...], preferred_element_type=F32)
    act = (_silu(gate) * up).astype(BF16)
    o_ref[...] += jnp.dot(act, wo_ref[...], preferred_element_type=F32)

    @pl.when(j == pl.num_programs(1) - 1)
    def _():
        o_ref[...] = x_ref[...] + 0.5 * _rms(o_ref[...], gout_ref[...])


def _ffn(x, gin, gout, w_in, w_out, layer, slot):
    t, d = x.shape
    fp = w_out.shape[2]
    nf = fp // FFN_TF
    return pl.pallas_call(
        _ffn_kernel,
        out_shape=jax.ShapeDtypeStruct((t, d), F32),
        grid=(t // FFN_TM, nf),
        in_specs=[
            pl.BlockSpec((FFN_TM, d), lambda i, j: (i, 0)),
            pl.BlockSpec((1, d), lambda i, j: (0, 0)),
            pl.BlockSpec((1, d), lambda i, j: (0, 0)),
            pl.BlockSpec((None, None, d, FFN_TF), lambda i, j: (layer, slot, 0, j)),
            pl.BlockSpec((None, None, d, FFN_TF), lambda i, j: (layer, slot, 0, nf + j)),
            pl.BlockSpec((None, None, FFN_TF, d), lambda i, j: (layer, slot, j, 0)),
        ],
        out_specs=pl.BlockSpec((FFN_TM, d), lambda i, j: (i, 0)),
        scratch_shapes=[pltpu.VMEM((FFN_TM, d), BF16)],
        compiler_params=_params("parallel", "arbitrary"),
        name="ffn",
    )(x, gin, gout, w_in, w_in, w_out)


def _rope_table_kernel(pos_ref, invf_ref, cos_ref, sin_ref):
    ang = pos_ref[...].astype(F32) * invf_ref[...]
    lane = lax.broadcasted_iota(jnp.int32, ang.shape, 1)
    first_half = (lane % HEAD_DIM) < (HEAD_DIM // 2)
    cos_ref[...] = jnp.cos(ang)
    sin_ref[...] = jnp.where(first_half, -jnp.sin(ang), jnp.sin(ang))


def _rope_tables(positions):
    t = positions.size
    tm = min(t, 2048)
    inv_freq = ROPE_THETA ** (-jnp.arange(0, HEAD_DIM, 2, dtype=F32) / HEAD_DIM)
    invf = jnp.tile(inv_freq, LANES // (HEAD_DIM // 2)).reshape(1, LANES)
    return pl.pallas_call(
        _rope_table_kernel,
        out_shape=[jax.ShapeDtypeStruct((t, LANES), F32)] * 2,
        grid=(t // tm,),
        in_specs=[pl.BlockSpec((tm, 1), lambda i: (i, 0)),
                  pl.BlockSpec((1, LANES), lambda i: (0, 0))],
        out_specs=[pl.BlockSpec((tm, LANES), lambda i: (i, 0))] * 2,
        compiler_params=_params("parallel"),
        name="rope_tables",
    )(positions.reshape(t, 1), invf)


def _rope(x, cos, sin, first_half):
    half = HEAD_DIM // 2
    partner = jnp.where(first_half, pltpu.roll(x, LANES - half, axis=1), pltpu.roll(x, half, axis=1))
    return x * cos + partner * sin


def _inproj_kernel(x_ref, g_ref, w_ref, w2_ref, gb_ref, cos_ref, sin_ref,
                   qa_ref, ka_ref, va_ref, qg_ref, kg_ref, vg_ref, rg_ref, la_ref):
    hn = _rms(x_ref[...], g_ref[...]).astype(BF16)
    cos = cos_ref[...]
    sin = sin_ref[...]
    lane = lax.broadcasted_iota(jnp.int32, cos.shape, 1)
    first_half = (lane % HEAD_DIM) < (HEAD_DIM // 2)

    def proj(c0, width):
        return jnp.dot(hn, w_ref[:, c0:c0 + width], preferred_element_type=F32)

    def rope_into(dst_ref, y):
        for s in range(y.shape[1] // LANES):
            cols = slice(s * LANES, (s + 1) * LANES)
            dst_ref[:, cols] = _rope(y[:, cols], cos, sin, first_half).astype(BF16)

    c = 0
    rope_into(qa_ref, proj(c, ATTN_WIDTH))
    c += ATTN_WIDTH
    kv = proj(c, 2 * KV_WIDTH)
    rope_into(ka_ref, kv[:, :KV_WIDTH])
    va_ref[...] = kv[:, KV_WIDTH:].astype(BF16)
    c += 2 * KV_WIDTH
    qg_ref[...] = proj(c, GLA_K_WIDTH)
    c += GLA_K_WIDTH
    kg_ref[...] = proj(c, GLA_K_WIDTH)
    c += GLA_K_WIDTH
    vg_ref[...] = proj(c, GLA_V_WIDTH).astype(BF16)
    c += GLA_V_WIDTH
    rg_ref[...] = proj(c, GLA_V_WIDTH).astype(BF16)
    c += GLA_V_WIDTH

    g_lr = proj(c, w_ref.shape[1] - c).astype(BF16)
    z = jnp.dot(g_lr, w2_ref[...], preferred_element_type=F32) + gb_ref[...]
    la_ref[...] = jax.nn.log_sigmoid(z) / GLA_TAU


def _inproj(x, gain, w_mix, w2, gate_b, cos, sin, layer):
    t, d = x.shape
    tm = PROJ_TM
    row = lambda width: pl.BlockSpec((tm, width), lambda i: (i, 0))
    widths = (ATTN_WIDTH, KV_WIDTH, KV_WIDTH, GLA_K_WIDTH, GLA_K_WIDTH, GLA_V_WIDTH, GLA_V_WIDTH,
              GLA_K_WIDTH)
    dtypes = (BF16, BF16, BF16, F32, F32, BF16, BF16, F32)
    return pl.pallas_call(
        _inproj_kernel,
        out_shape=[jax.ShapeDtypeStruct((t, w), dt) for w, dt in zip(widths, dtypes)],
        grid=(t // tm,),
        in_specs=[row(d), _resident((1, d)), _resident(w_mix.shape[1:], layer),
                  _resident(w2.shape[1:], layer), _resident((1, GLA_K_WIDTH)),
                  row(LANES), row(LANES)],
        out_specs=[row(w) for w in widths],
        compiler_params=_params("parallel"),
        name="mixer_inproj",
    )(x, gain, w_mix, w2, gate_b, cos, sin)


def _attn_kernel(sinks_ref, q_ref, kc_ref, vc_ref, kp_ref, vp_ref, o_ref):
    n = pl.program_id(1)
    blk = ATTN_BLOCK
    k_all = jnp.concatenate([kp_ref[...], kc_ref[...]], axis=0)
    v_all = jnp.concatenate([vp_ref[...], vc_ref[...]], axis=0)
    qi = lax.broadcasted_iota(jnp.int32, (blk, 2 * blk), 0)
    kj = lax.broadcasted_iota(jnp.int32, (blk, 2 * blk), 1)
    diff = qi + blk - kj
    band = (diff >= 0) & (diff < blk)
    first_valid = jnp.where(n > 0, 0, blk)
    scale = HEAD_DIM ** -0.5
    for b in range(q_ref.shape[0] // blk):
        rows = slice(b * blk, (b + 1) * blk)
        mask = band & (kj >= first_valid) if b == 0 else band
        outs = []
        for h in range(N_Q_HEADS):
            g = h // Q_PER_KV
            q = q_ref[rows, h * HEAD_DIM:(h + 1) * HEAD_DIM]
            k = k_all[b * blk:(b + 2) * blk, g * HEAD_DIM:(g + 1) * HEAD_DIM]
            v = v_all[b * blk:(b + 2) * blk, g * HEAD_DIM:(g + 1) * HEAD_DIM]
            s = lax.dot_general(q, k, (((1,), (1,)), ((), ())), preferred_element_type=F32) * scale
            s = jnp.where(mask, s, -jnp.inf)
            sink = sinks_ref[h]
            m = jnp.maximum(jnp.max(s, axis=-1, keepdims=True), sink)
            p = jnp.exp(s - m)
            denom = jnp.sum(p, axis=-1, keepdims=True) + jnp.exp(sink - m)
            pv = jnp.dot(p.astype(BF16), v, preferred_element_type=F32)
            outs.append(pv / denom)
        o_ref[rows, :] = jnp.concatenate(outs, axis=1).astype(BF16)


def _attention(q, k, v, sinks, batch):
    t = q.shape[0]
    seq = t // batch
    tq = min(ATTN_TQ, seq)
    nq = seq // tq
    per = tq // ATTN_BLOCK
    cur = lambda width: pl.BlockSpec((tq, width), lambda b, n: (b * nq + n, 0))
    prev = lambda width: pl.BlockSpec(
        (ATTN_BLOCK, width), lambda b, n: (jnp.maximum((b * nq + n) * per - 1, 0), 0))
    return pl.pallas_call(
        _attn_kernel,
        out_shape=jax.ShapeDtypeStruct((t, ATTN_WIDTH), BF16),
        grid=(batch, nq),
        in_specs=[pl.BlockSpec(memory_space=pltpu.SMEM),
                  cur(ATTN_WIDTH), cur(KV_WIDTH), cur(KV_WIDTH), prev(KV_WIDTH), prev(KV_WIDTH)],
        out_specs=cur(ATTN_WIDTH),
        compiler_params=_params("parallel", "arbitrary"),
        name="swa_attention",
    )(sinks, q, k, v, k, v)


def _gla_kernel(q_ref, k_ref, v_ref, la_ref, r_ref, gain_ref, o_ref, state_ref):
    @pl.when(pl.program_id(1) == 0)
    def _():
        state_ref[...] = jnp.zeros_like(state_ref)

    c = GLA_CHUNK
    ti = lax.broadcasted_iota(jnp.int32, (c, c), 0)
    si = lax.broadcasted_iota(jnp.int32, (c, c), 1)
    causal = ti >= si
    tri = causal.astype(F32)
    gain = gain_ref[...]
    scale = GLA_DK ** -0.5
    for ci in range(q_ref.shape[0] // c):
        rows = slice(ci * c, (ci + 1) * c)
        la = la_ref[rows, :]
        b = jnp.dot(tri, la, precision=lax.Precision.HIGHEST, preferred_element_type=F32)
        b_last = b[c - 1:c, :]
        q_dec = (q_ref[rows, :] * scale) * jnp.exp(b)
        k_all = k_ref[rows, :]
        k_dec = k_all * jnp.exp(-b)
        k_rem = k_all * jnp.exp(b_last - b)
        chunk_decay = jnp.exp(b_last)
        for h in range(GLA_HEADS):
            kl = slice(h * GLA_DK, (h + 1) * GLA_DK)
            vl = slice(h * GLA_DV, (h + 1) * GLA_DV)
            qh = q_dec[:, kl].astype(BF16)
            vh = v_ref[rows, vl]
            intra = lax.dot_general(qh, k_dec[:, kl].astype(BF16), (((1,), (1,)), ((), ())),
                                    preferred_element_type=F32)
            intra = jnp.where(causal, intra, 0.0).astype(BF16)
            state = state_ref[h]
            o = (jnp.dot(intra, vh, preferred_element_type=F32)
                 + jnp.dot(qh, state.astype(BF16), preferred_element_type=F32))
            d_state = jnp.dot(k_rem[:, kl].T.astype(BF16), vh, preferred_element_type=F32)
            decay_col = jnp.broadcast_to(chunk_decay[:, kl], (GLA_DK, GLA_DK)).T
            state_ref[h] = jnp.concatenate([decay_col] * (GLA_DV // GLA_DK), axis=1) * state + d_state
            r = r_ref[rows, vl].astype(F32)
            o_ref[rows, vl] = (_rms(o, gain) * _silu(r)).astype(BF16)


def _gla(q, k, v, log_a, r, gain, batch):
    t = q.shape[0]
    seq = t // batch
    ts = min(GLA_TS, seq)
    ns = seq // ts
    row = lambda width: pl.BlockSpec((ts, width), lambda b, n: (b * ns + n, 0))
    return pl.pallas_call(
        _gla_kernel,
        out_shape=jax.ShapeDtypeStruct((t, GLA_V_WIDTH), BF16),
        grid=(batch, ns),
        in_specs=[row(GLA_K_WIDTH), row(GLA_K_WIDTH), row(GLA_V_WIDTH), row(GLA_K_WIDTH),
                  row(GLA_V_WIDTH), pl.BlockSpec((1, GLA_DV), lambda b, n: (0, 0))],
        out_specs=row(GLA_V_WIDTH),
        scratch_shapes=[pltpu.VMEM((GLA_HEADS, GLA_DK, GLA_DV), F32)],
        compiler_params=_params("parallel", "arbitrary"),
        name="gla",
    )(q, k, v, log_a, r, gain)


def _outproj_kernel(x_ref, a_ref, g_ref, w_ref, gain_ref, o_ref):
    h = (jnp.dot(a_ref[...], w_ref[:ATTN_WIDTH, :], preferred_element_type=F32)
         + jnp.dot(g_ref[...], w_ref[ATTN_WIDTH:, :], preferred_element_type=F32))
    o_ref[...] = x_ref[...] + _rms(h, gain_ref[...])


def _outproj(x, attn, gla, w_out, gain, layer):
    t, d = x.shape
    tm = PROJ_TM
    row = lambda width: pl.BlockSpec((tm, width), lambda i: (i, 0))
    return pl.pallas_call(
        _outproj_kernel,
        out_shape=jax.ShapeDtypeStruct((t, d), F32),
        grid=(t // tm,),
        in_specs=[row(d), row(ATTN_WIDTH), row(GLA_V_WIDTH), _resident(w_out.shape[1:], layer),
                  _resident((1, d))],
        out_specs=row(d),
        compiler_params=_params("parallel"),
        name="mixer_outproj",
    )(x, attn, gla, w_out, gain)


def _pad_to(a, axis, multiple):
    pad = (-a.shape[axis]) % multiple
    widths = [(0, 0)] * a.ndim
    widths[axis] = (0, pad)
    return jnp.pad(a, widths)


def _prep_weights(ffn_w_in, ffn_w_out, w_mix_in, gla_gate_w2, w_mix_out):
    d_ff = ffn_w_out.shape[-2]
    gate = _pad_to(ffn_w_in[..., :d_ff], -1, FFN_TF)
    up = _pad_to(ffn_w_in[..., d_ff:], -1, FFN_TF)
    w_in = jnp.concatenate([gate, up], axis=-1).astype(BF16)
    w_out = _pad_to(ffn_w_out, -2, FFN_TF).astype(BF16)
    w_mix = _pad_to(w_mix_in, -1, 2 * LANES).astype(BF16)
    main = w_mix_in.shape[-1] - GLA_GATE_RANK
    w2 = jnp.pad(gla_gate_w2, ((0, 0), (0, w_mix.shape[-1] - main - GLA_GATE_RANK), (0, 0))).astype(BF16)
    return w_in, w_out, w_mix, w2, w_mix_out.astype(BF16)


def kernel(x, positions, norm_gains, ffn_w_in, ffn_w_out, w_mix_in, attn_sinks, gla_gate_w2,
           gla_gate_b, gla_norm_gain, w_mix_out):
    batch, seq, d = x.shape
    depth = norm_gains.shape[0]
    xt = x.reshape(batch * seq, d)
    cos, sin = _rope_tables(positions)
    w_in, w_out, w_mix, w2, w_mo = _prep_weights(ffn_w_in, ffn_w_out, w_mix_in, gla_gate_w2, w_mix_out)
    for l in range(depth):
        g = norm_gains[l].reshape(-1, 1, d)
        xt = _ffn(xt, g[0], g[1], w_in, w_out, l, 0)
        qa, ka, va, qg, kg, vg, rg, la = _inproj(
            xt, g[2], w_mix, w2, gla_gate_b[l].reshape(1, -1), cos, sin, l)
        attn = _attention(qa, ka, va, attn_sinks[l], batch)
        gla = _gla(qg, kg, vg, la, rg, gla_norm_gain[l].reshape(1, -1), batch)
        xt = _outproj(xt, attn, gla, w_mo, g[3], l)
        xt = _ffn(xt, g[4], g[5], w_in, w_out, l, 1)
    return xt.reshape(batch, seq, d)
```

```python
import functools

import numpy as np
import jax
import jax.numpy as jnp
from jax import lax
from jax.experimental import pallas as pl
from jax.experimental.pallas import tpu as pltpu

F32 = jnp.float32
BF16 = jnp.bfloat16

HEAD_DIM = 64
N_Q_HEADS = 16
N_KV_HEADS = 4
Q_PER_KV = N_Q_HEADS // N_KV_HEADS
ATTN_WIDTH = N_Q_HEADS * HEAD_DIM
KV_WIDTH = N_KV_HEADS * HEAD_DIM
ATTN_BLOCK = 128
ROPE_THETA = 10000.0
GLA_HEADS = 4
GLA_DK = 128
GLA_DV = 256
GLA_K_WIDTH = GLA_HEADS * GLA_DK
GLA_V_WIDTH = GLA_HEADS * GLA_DV
GLA_GATE_RANK = 16
GLA_TAU = 16.0
GLA_CHUNK = 64
NORM_EPS = 1e-6

LANES = 128
VMEM_LIMIT_BYTES = 60000 * 1024

FFN_TM = 512
FFN_TF = 512
PROJ_TM = 512
ATTN_TQ = 256
GLA_TS = 256


def _params(*sem):
    return pltpu.CompilerParams(dimension_semantics=sem, vmem_limit_bytes=VMEM_LIMIT_BYTES)


def _resident(shape, lead=()):
    lead = tuple(lead) if isinstance(lead, (tuple, list)) else (lead,)
    return pl.BlockSpec((None,) * len(lead) + tuple(shape), lambda *_: lead + (0,) * len(shape),
                        pipeline_mode=pl.Buffered(1))


def _rms(x, gain):
    ms = jnp.mean(x * x, axis=-1, keepdims=True)
    return (x * lax.rsqrt(ms + NORM_EPS)) * gain


def _silu(x):
    return x * jax.nn.sigmoid(x)


def _swiglu_chunk(hn, wg, wu, wo):
    gate = jnp.dot(hn, wg, preferred_element_type=F32)
    up = jnp.dot(hn, wu, preferred_element_type=F32)
    act = (_silu(gate) * up).astype(BF16)
    return jnp.dot(act, wo, preferred_element_type=F32)


def _ffn_kernel(x_ref, gin_ref, gout_ref, wg_ref, wu_ref, wo_ref, wgt_ref, wut_ref, wot_ref,
                o_ref, hn_ref):
    j = pl.program_id(1)
    last = pl.num_programs(1) - 1

    @pl.when(j == 0)
    def _():
        hn = _rms(x_ref[...], gin_ref[...]).astype(BF16)
        hn_ref[...] = hn
        o_ref[...] = _swiglu_chunk(hn, wg_ref[...], wu_ref[...], wo_ref[...])

    @pl.when((j > 0) & (j < last))
    def _():
        o_ref[...] += _swiglu_chunk(hn_ref[...], wg_ref[...], wu_ref[...], wo_ref[...])

    @pl.when(j == last)
    def _():
        half = o_ref.shape[0] // 2
        for r in range(2):
            rows = slice(r * half, (r + 1) * half)
            h = o_ref[rows, :] + _swiglu_chunk(hn_ref[rows, :], wgt_ref[...], wut_ref[...], wot_ref[...])
            o_ref[rows, :] = x_ref[rows, :] + 0.5 * _rms(h, gout_ref[...])


def _ffn(x, gin, gout, w, layer, slot):
    wg, wu, wo, wgt, wut, wot = w
    t, d = x.shape
    n_full = wo.shape[2] // FFN_TF
    assert n_full >= 2 and wg.shape[3] == wu.shape[3] == n_full * FFN_TF
    main = lambda i, j: jnp.minimum(j, n_full - 1)
    return pl.pallas_call(
        _ffn_kernel,
        out_shape=jax.ShapeDtypeStruct((t, d), F32),
        grid=(t // FFN_TM, n_full + 1),
        in_specs=[
            pl.BlockSpec((FFN_TM, d), lambda i, j: (i, 0)),
            pl.BlockSpec((1, d), lambda i, j: (0, 0)),
            pl.BlockSpec((1, d), lambda i, j: (0, 0)),
            pl.BlockSpec((None, None, d, FFN_TF), lambda i, j: (layer, slot, 0, main(i, j))),
            pl.BlockSpec((None, None, d, FFN_TF), lambda i, j: (layer, slot, 0, main(i, j))),
            pl.BlockSpec((None, None, FFN_TF, d), lambda i, j: (layer, slot, main(i, j), 0)),
            _resident((d, FFN_TF), (layer, slot)),
            _resident((d, FFN_TF), (layer, slot)),
            _resident((FFN_TF, d), (layer, slot)),
        ],
        out_specs=pl.BlockSpec((FFN_TM, d), lambda i, j: (i, 0)),
        scratch_shapes=[pltpu.VMEM((FFN_TM, d), BF16)],
        compiler_params=_params("parallel", "arbitrary"),
        name="ffn",
    )(x, gin, gout, wg, wu, wo, wgt, wut, wot)


def _rope_table_kernel(pos_ref, invf_ref, cos_ref, sin_ref):
    ang = pos_ref[...].astype(F32) * invf_ref[...]
    lane = lax.broadcasted_iota(jnp.int32, ang.shape, 1)
    first_half = (lane % HEAD_DIM) < (HEAD_DIM // 2)
    cos_ref[...] = jnp.cos(ang)
    sin_ref[...] = jnp.where(first_half, -jnp.sin(ang), jnp.sin(ang))


def _rope_tables(positions):
    t = positions.size
    tm = min(t, 2048)
    inv_freq = ROPE_THETA ** (-jnp.arange(0, HEAD_DIM, 2, dtype=F32) / HEAD_DIM)
    invf = jnp.tile(inv_freq, LANES // (HEAD_DIM // 2)).reshape(1, LANES)
    return pl.pallas_call(
        _rope_table_kernel,
        out_shape=[jax.ShapeDtypeStruct((t, LANES), F32)] * 2,
        grid=(t // tm,),
        in_specs=[pl.BlockSpec((tm, 1), lambda i: (i, 0)),
                  pl.BlockSpec((1, LANES), lambda i: (0, 0))],
        out_specs=[pl.BlockSpec((tm, LANES), lambda i: (i, 0))] * 2,
        compiler_params=_params("parallel"),
        name="rope_tables",
    )(positions.reshape(t, 1), invf)


def _rope(x, cos, sin, first_half):
    half = HEAD_DIM // 2
    partner = jnp.where(first_half, pltpu.roll(x, LANES - half, axis=1), pltpu.roll(x, half, axis=1))
    return x * cos + partner * sin


def _inproj_kernel(x_ref, g_ref, w_ref, w2_ref, gb_ref, cos_ref, sin_ref,
                   qa_ref, ka_ref, va_ref, qg_ref, kg_ref, vg_ref, rg_ref, la_ref):
    hn = _rms(x_ref[...], g_ref[...]).astype(BF16)
    cos = cos_ref[...]
    sin = sin_ref[...]
    lane = lax.broadcasted_iota(jnp.int32, cos.shape, 1)
    first_half = (lane % HEAD_DIM) < (HEAD_DIM // 2)

    def proj(c0, width):
        return jnp.dot(hn, w_ref[:, c0:c0 + width], preferred_element_type=F32)

    def dup_into(dst_ref, s, slab):
        lo = lane < HEAD_DIM
        swapped = pltpu.roll(slab, HEAD_DIM, axis=1)
        dst_ref[:, (2 * s) * LANES:(2 * s + 1) * LANES] = jnp.where(lo, slab, swapped).astype(BF16)
        dst_ref[:, (2 * s + 1) * LANES:(2 * s + 2) * LANES] = jnp.where(lo, swapped, slab).astype(BF16)

    c = 0
    qa = proj(c, ATTN_WIDTH)
    for s in range(ATTN_WIDTH // LANES):
        cols = slice(s * LANES, (s + 1) * LANES)
        qa_ref[:, cols] = (_rope(qa[:, cols], cos, sin, first_half) * HEAD_DIM ** -0.5).astype(BF16)
    c += ATTN_WIDTH
    kv = proj(c, 2 * KV_WIDTH)
    for s in range(KV_WIDTH // LANES):
        dup_into(ka_ref, s, _rope(kv[:, s * LANES:(s + 1) * LANES], cos, sin, first_half))
        dup_into(va_ref, s, kv[:, KV_WIDTH + s * LANES:KV_WIDTH + (s + 1) * LANES])
    c += 2 * KV_WIDTH
    qg_ref[...] = proj(c, GLA_K_WIDTH)
    c += GLA_K_WIDTH
    kg_ref[...] = proj(c, GLA_K_WIDTH)
    c += GLA_K_WIDTH
    vg_ref[...] = proj(c, GLA_V_WIDTH).astype(BF16)
    c += GLA_V_WIDTH
    rg_ref[...] = proj(c, GLA_V_WIDTH).astype(BF16)
    c += GLA_V_WIDTH

    g_lr = proj(c, w_ref.shape[1] - c).astype(BF16)
    z = jnp.dot(g_lr, w2_ref[...], preferred_element_type=F32) + gb_ref[...]
    la_ref[...] = jax.nn.log_sigmoid(z) / GLA_TAU


def _inproj(x, gain, w_mix, w2, gate_b, cos, sin, layer):
    t, d = x.shape
    tm = PROJ_TM
    row = lambda width: pl.BlockSpec((tm, width), lambda i: (i, 0))
    widths = (ATTN_WIDTH, 2 * KV_WIDTH, 2 * KV_WIDTH, GLA_K_WIDTH, GLA_K_WIDTH, GLA_V_WIDTH,
              GLA_V_WIDTH, GLA_K_WIDTH)
    dtypes = (BF16, BF16, BF16, F32, F32, BF16, BF16, F32)
    return pl.pallas_call(
        _inproj_kernel,
        out_shape=[jax.ShapeDtypeStruct((t, w), dt) for w, dt in zip(widths, dtypes)],
        grid=(t // tm,),
        in_specs=[row(d), _resident((1, d)), _resident(w_mix.shape[1:], layer),
                  _resident(w2.shape[1:], layer), _resident((1, GLA_K_WIDTH)),
                  row(LANES), row(LANES)],
        out_specs=[row(w) for w in widths],
        compiler_params=_params("parallel"),
        name="mixer_inproj",
    )(x, gain, w_mix, w2, gate_b, cos, sin)


def _attn_kernel(sinks_ref, q_ref, kc_ref, vc_ref, kp_ref, vp_ref, o_ref):
    n = pl.program_id(1)
    blk = ATTN_BLOCK
    k_all = jnp.concatenate([kp_ref[...], kc_ref[...]], axis=0)
    v_all = jnp.concatenate([vp_ref[...], vc_ref[...]], axis=0)
    qi = lax.broadcasted_iota(jnp.int32, (2 * blk, blk), 0) % blk
    kj = lax.broadcasted_iota(jnp.int32, (2 * blk, blk), 1)
    top = lax.broadcasted_iota(jnp.int32, (2 * blk, 1), 0) < blk
    band_prev = kj > qi
    band_cur = kj <= qi
    band_prev_first = band_prev & (kj >= jnp.where(n > 0, 0, blk))
    lane = lax.broadcasted_iota(jnp.int32, (2 * blk, LANES), 1)
    key = lax.broadcasted_iota(jnp.int32, (2 * blk, LANES), 0)
    half = (lane < HEAD_DIM, lane >= HEAD_DIM)
    ones_half = tuple(jnp.where(hm, 1.0, 0.0).astype(BF16) for hm in half)
    zero = jnp.zeros((2 * blk, LANES), BF16)
    neg_inf = jnp.full((2 * blk, blk), -jnp.inf, F32)
    for g in range(N_KV_HEADS):
        gl = slice(g * LANES, (g + 1) * LANES)
        fills = []
        for parity in range(2):
            sink = jnp.where(top, sinks_ref[Q_PER_KV * g + parity], sinks_ref[Q_PER_KV * g + 2 + parity])
            fills.append(jnp.where(kj == 0, sink, neg_inf))
        for b in range(q_ref.shape[0] // blk):
            rows = slice(b * blk, (b + 1) * blk)
            q = jnp.concatenate([q_ref[rows, (2 * g) * LANES:(2 * g + 1) * LANES],
                                 q_ref[rows, (2 * g + 1) * LANES:(2 * g + 2) * LANES]], axis=0)
            kd = k_all[b * blk:(b + 2) * blk, gl]
            vd = v_all[b * blk:(b + 2) * blk, gl]
            mask_prev = band_prev_first if b == 0 else band_prev
            acc = None
            for parity in range(2):
                kz = jnp.where(half[parity], kd, zero)
                s = lax.dot_general(q, kz, (((1,), (1,)), ((), ())), preferred_element_type=F32)
                s_prev = jnp.where(mask_prev, s[:, :blk], fills[parity])
                s_cur = jnp.where(band_cur, s[:, blk:], neg_inf)
                m = jnp.max(jnp.maximum(s_prev, s_cur), axis=-1, keepdims=True)
                p = jnp.concatenate([jnp.exp(s_prev - m), jnp.exp(s_cur - m)], axis=1).astype(BF16)
                vz = jnp.where(half[parity] & (key > 0), vd, zero)
                w = jnp.concatenate([vz, ones_half[parity]], axis=1)
                part = jnp.dot(p, w, preferred_element_type=F32)
                acc = part if acc is None else acc + part
            out = (acc[:, :LANES] / acc[:, LANES:]).astype(BF16)
            o_ref[rows, (2 * g) * LANES:(2 * g + 1) * LANES] = out[:blk]
            o_ref[rows, (2 * g + 1) * LANES:(2 * g + 2) * LANES] = out[blk:]


def _attention(q, k, v, sinks, batch):
    t = q.shape[0]
    seq = t // batch
    tq = min(ATTN_TQ, seq)
    nq = seq // tq
    per = tq // ATTN_BLOCK
    kvw = k.shape[1]
    cur = lambda width: pl.BlockSpec((tq, width), lambda b, n: (b * nq + n, 0))
    prev = lambda width: pl.BlockSpec(
        (ATTN_BLOCK, width), lambda b, n: (jnp.maximum((b * nq + n) * per - 1, 0), 0))
    return pl.pallas_call(
        _attn_kernel,
        out_shape=jax.ShapeDtypeStruct((t, ATTN_WIDTH), BF16),
        grid=(batch, nq),
        in_specs=[pl.BlockSpec(memory_space=pltpu.SMEM),
                  cur(ATTN_WIDTH), cur(kvw), cur(kvw), prev(kvw), prev(kvw)],
        out_specs=cur(ATTN_WIDTH),
        compiler_params=_params("parallel", "arbitrary"),
        name="swa_attention",
    )(sinks, q, k, v, k, v)


def _gla_kernel(q_ref, k_ref, v_ref, la_ref, r_ref, gain_ref, o_ref, state_ref):
    @pl.when(pl.program_id(1) == 0)
    def _():
        state_ref[...] = jnp.zeros_like(state_ref)

    c = GLA_CHUNK
    ti = lax.broadcasted_iota(jnp.int32, (c, c), 0)
    si = lax.broadcasted_iota(jnp.int32, (c, c), 1)
    causal = ti >= si
    tri = causal.astype(F32)
    gain = gain_ref[...]
    scale = GLA_DK ** -0.5
    nt = (((1,), (1,)), ((), ()))
    state_t = [state_ref[h] for h in range(GLA_HEADS)]
    for ci in range(q_ref.shape[0] // c):
        rows = slice(ci * c, (ci + 1) * c)
        la = la_ref[rows, :]
        b = jnp.dot(tri, la, precision=lax.Precision.HIGHEST, preferred_element_type=F32)
        b_last = b[c - 1:c, :]
        q_dec = (q_ref[rows, :] * scale) * jnp.exp(b)
        k_all = k_ref[rows, :]
        k_dec = k_all * jnp.exp(-b)
        k_rem = k_all * jnp.exp(b_last - b)
        chunk_decay = jnp.exp(b_last)
        for h in range(GLA_HEADS):
            kl = slice(h * GLA_DK, (h + 1) * GLA_DK)
            vl = slice(h * GLA_DV, (h + 1) * GLA_DV)
            qh = q_dec[:, kl].astype(BF16)
            v_t = v_ref[rows, vl].astype(F32).T.astype(BF16)
            intra = lax.dot_general(qh, k_dec[:, kl].astype(BF16), nt, preferred_element_type=F32)
            intra = jnp.where(causal, intra, 0.0).astype(BF16)
            o = lax.dot_general(jnp.concatenate([qh, intra], axis=1),
                                jnp.concatenate([state_t[h].astype(BF16), v_t], axis=1),
                                nt, preferred_element_type=F32)
            d_state_t = jnp.dot(v_t, k_rem[:, kl].astype(BF16), preferred_element_type=F32)
            state_t[h] = state_t[h] * chunk_decay[:, kl] + d_state_t
            r = r_ref[rows, vl].astype(F32)
            o_ref[rows, vl] = (_rms(o, gain) * _silu(r)).astype(BF16)
    for h in range(GLA_HEADS):
        state_ref[h] = state_t[h]


def _gla(q, k, v, log_a, r, gain, batch):
    t = q.shape[0]
    seq = t // batch
    ts = min(GLA_TS, seq)
    ns = seq // ts
    row = lambda width: pl.BlockSpec((ts, width), lambda b, n: (b * ns + n, 0))
    return pl.pallas_call(
        _gla_kernel,
        out_shape=jax.ShapeDtypeStruct((t, GLA_V_WIDTH), BF16),
        grid=(batch, ns),
        in_specs=[row(GLA_K_WIDTH), row(GLA_K_WIDTH), row(GLA_V_WIDTH), row(GLA_K_WIDTH),
                  row(GLA_V_WIDTH), pl.BlockSpec((1, GLA_DV), lambda b, n: (0, 0))],
        out_specs=row(GLA_V_WIDTH),
        scratch_shapes=[pltpu.VMEM((GLA_HEADS, GLA_DV, GLA_DK), F32)],
        compiler_params=_params("parallel", "arbitrary"),
        name="gla",
    )(q, k, v, log_a, r, gain)


def _outproj_kernel(x_ref, a_ref, g_ref, w_ref, gain_ref, o_ref):
    h = (jnp.dot(a_ref[...], w_ref[:ATTN_WIDTH, :], preferred_element_type=F32)
         + jnp.dot(g_ref[...], w_ref[ATTN_WIDTH:, :], preferred_element_type=F32))
    o_ref[...] = x_ref[...] + _rms(h, gain_ref[...])


def _outproj(x, attn, gla, w_out, gain, layer):
    t, d = x.shape
    tm = PROJ_TM
    row = lambda width: pl.BlockSpec((tm, width), lambda i: (i, 0))
    return pl.pallas_call(
        _outproj_kernel,
        out_shape=jax.ShapeDtypeStruct((t, d), F32),
        grid=(t // tm,),
        in_specs=[row(d), row(ATTN_WIDTH), row(GLA_V_WIDTH), _resident(w_out.shape[1:], layer),
                  _resident((1, d))],
        out_specs=row(d),
        compiler_params=_params("parallel"),
        name="mixer_outproj",
    )(x, attn, gla, w_out, gain)


def _pad_to(a, axis, multiple):
    pad = (-a.shape[axis]) % multiple
    widths = [(0, 0)] * a.ndim
    widths[axis] = (0, pad)
    return jnp.pad(a, widths)


def _prep_ffn_weights(ffn_w_in, ffn_w_out):
    d_ff = ffn_w_out.shape[-2]
    full = (d_ff // FFN_TF) * FFN_TF
    wg = ffn_w_in[..., :full].astype(BF16)
    wu = ffn_w_in[..., d_ff:d_ff + full].astype(BF16)
    wo = ffn_w_out[..., :full, :].astype(BF16)
    wgt = _pad_to(ffn_w_in[..., full:d_ff], -1, FFN_TF).astype(BF16)
    wut = _pad_to(ffn_w_in[..., d_ff + full:], -1, FFN_TF).astype(BF16)
    wot = _pad_to(ffn_w_out[..., full:, :], -2, FFN_TF).astype(BF16)
    return wg, wu, wo, wgt, wut, wot


def _prep_mixer_weights(w_mix_in, gla_gate_w2, w_mix_out):
    w_mix = _pad_to(w_mix_in, -1, 2 * LANES).astype(BF16)
    main = w_mix_in.shape[-1] - GLA_GATE_RANK
    w2 = jnp.pad(gla_gate_w2, ((0, 0), (0, w_mix.shape[-1] - main - GLA_GATE_RANK), (0, 0))).astype(BF16)
    return w_mix, w2, w_mix_out.astype(BF16)


def kernel(x, positions, norm_gains, ffn_w_in, ffn_w_out, w_mix_in, attn_sinks, gla_gate_w2,
           gla_gate_b, gla_norm_gain, w_mix_out):
    batch, seq, d = x.shape
    depth = norm_gains.shape[0]
    xt = x.reshape(batch * seq, d)
    cos, sin = _rope_tables(positions)
    w_ffn = _prep_ffn_weights(ffn_w_in, ffn_w_out)
    w_mix, w2, w_mo = _prep_mixer_weights(w_mix_in, gla_gate_w2, w_mix_out)
    for l in range(depth):
        g = norm_gains[l].reshape(-1, 1, d)
        xt = _ffn(xt, g[0], g[1], w_ffn, l, 0)
        qa, ka, va, qg, kg, vg, rg, la = _inproj(
            xt, g[2], w_mix, w2, gla_gate_b[l].reshape(1, -1), cos, sin, l)
        attn = _attention(qa, ka, va, attn_sinks[l], batch)
        gla = _gla(qg, kg, vg, la, rg, gla_norm_gain[l].reshape(1, -1), batch)
        xt = _outproj(xt, attn, gla, w_mo, g[3], l)
        xt = _ffn(xt, g[4], g[5], w_ffn, l, 1)
    return xt.reshape(batch, seq, d)
```

```python
import functools

import numpy as np
import jax
import jax.numpy as jnp
from jax import lax
from jax.experimental import pallas as pl
from jax.experimental.pallas import tpu as pltpu

F32 = jnp.float32
BF16 = jnp.bfloat16

HEAD_DIM = 64
N_Q_HEADS = 16
N_KV_HEADS = 4
Q_PER_KV = N_Q_HEADS // N_KV_HEADS
ATTN_WIDTH = N_Q_HEADS * HEAD_DIM
KV_WIDTH = N_KV_HEADS * HEAD_DIM
ATTN_BLOCK = 128
ROPE_THETA = 10000.0
GLA_HEADS = 4
GLA_DK = 128
GLA_DV = 256
GLA_K_WIDTH = GLA_HEADS * GLA_DK
GLA_V_WIDTH = GLA_HEADS * GLA_DV
GLA_GATE_RANK = 16
GLA_TAU = 16.0
GLA_CHUNK = 64
NORM_EPS = 1e-6

LANES = 128
VMEM_LIMIT_BYTES = 60000 * 1024

FFN_TM = 512
FFN_TF = 512
PROJ_TM = 512
ATTN_TQ = 256
GLA_TS = 256
CAST_ROWS = 256


def _params(*sem):
    return pltpu.CompilerParams(dimension_semantics=sem, vmem_limit_bytes=VMEM_LIMIT_BYTES)


def _resident(shape, lead=()):
    lead = tuple(lead) if isinstance(lead, (tuple, list)) else (lead,)
    return pl.BlockSpec((None,) * len(lead) + tuple(shape), lambda *_: lead + (0,) * len(shape),
                        pipeline_mode=pl.Buffered(1))


def _rms(x, gain):
    ms = jnp.mean(x * x, axis=-1, keepdims=True)
    return (x * lax.rsqrt(ms + NORM_EPS)) * gain


def _silu(x):
    return x * jax.nn.sigmoid(x)


def _swiglu_chunk(hn, wg, wu, wo):
    gate = jnp.dot(hn, wg, preferred_element_type=F32)
    up = jnp.dot(hn, wu, preferred_element_type=F32)
    act = (_silu(gate) * up).astype(BF16)
    return jnp.dot(act, wo, preferred_element_type=F32)


def _ffn_kernel(x_ref, gin_ref, gout_ref, wg_ref, wu_ref, wo_ref, o_ref, hn_ref):
    j = pl.program_id(1)
    last = pl.num_programs(1) - 1

    @pl.when(j == 0)
    def _():
        hn = _rms(x_ref[...], gin_ref[...]).astype(BF16)
        hn_ref[...] = hn
        o_ref[...] = _swiglu_chunk(hn, wg_ref[...], wu_ref[...], wo_ref[...])

    @pl.when((j > 0) & (j < last))
    def _():
        o_ref[...] += _swiglu_chunk(hn_ref[...], wg_ref[...], wu_ref[...], wo_ref[...])

    @pl.when(j == last)
    def _():
        half = o_ref.shape[0] // 2
        for r in range(2):
            rows = slice(r * half, (r + 1) * half)
            h = o_ref[rows, :] + _swiglu_chunk(hn_ref[rows, :], wg_ref[...], wu_ref[...], wo_ref[...])
            o_ref[rows, :] = x_ref[rows, :] + 0.5 * _rms(h, gout_ref[...])


def _ffn(x, gin, gout, w, layer, slot):
    wg, wu, wo = w
    t, d = x.shape
    nc = wg.shape[2]
    assert nc >= 2 and wo.shape[2] == nc * FFN_TF
    chunk = pl.BlockSpec((None, None, None, d, FFN_TF), lambda i, j: (layer, slot, j, 0, 0))
    return pl.pallas_call(
        _ffn_kernel,
        out_shape=jax.ShapeDtypeStruct((t, d), F32),
        grid=(t // FFN_TM, nc),
        in_specs=[
            pl.BlockSpec((FFN_TM, d), lambda i, j: (i, 0)),
            pl.BlockSpec((1, d), lambda i, j: (0, 0)),
            pl.BlockSpec((1, d), lambda i, j: (0, 0)),
            chunk,
            chunk,
            pl.BlockSpec((None, None, FFN_TF, d), lambda i, j: (layer, slot, j, 0)),
        ],
        out_specs=pl.BlockSpec((FFN_TM, d), lambda i, j: (i, 0)),
        scratch_shapes=[pltpu.VMEM((FFN_TM, d), BF16)],
        compiler_params=_params("parallel", "arbitrary"),
        name="ffn",
    )(x, gin, gout, wg, wu, wo)


def _cast_w_in_kernel(w_ref, g_ref, u_ref):
    d_ff = w_ref.shape[1] // 2
    for half, dst in ((0, g_ref), (1, u_ref)):
        for c in range(dst.shape[0]):
            lo = c * FFN_TF
            width = min(FFN_TF, d_ff - lo)
            dst[c, :, :width] = w_ref[:, half * d_ff + lo:half * d_ff + lo + width].astype(BF16)
            if width < FFN_TF:
                dst[c, :, width:] = jnp.zeros((dst.shape[1], FFN_TF - width), BF16)


def _cast_w_in(ffn_w_in):
    depth, two, d, f2 = ffn_w_in.shape
    nc = pl.cdiv(f2 // 2, FFN_TF)
    rt = CAST_ROWS
    out = jax.ShapeDtypeStruct((depth, two, nc, d, FFN_TF), BF16)
    out_spec = pl.BlockSpec((None, None, nc, rt, FFN_TF), lambda m, i: (m // two, m % two, 0, i, 0))
    return pl.pallas_call(
        _cast_w_in_kernel,
        out_shape=[out, out],
        grid=(depth * two, d // rt),
        in_specs=[pl.BlockSpec((None, None, rt, f2), lambda m, i: (m // two, m % two, i, 0))],
        out_specs=[out_spec, out_spec],
        compiler_params=_params("parallel", "parallel"),
        name="cast_ffn_w_in",
    )(ffn_w_in)


def _rope_table_kernel(pos_ref, invf_ref, cos_ref, sin_ref):
    ang = pos_ref[...].astype(F32) * invf_ref[...]
    lane = lax.broadcasted_iota(jnp.int32, ang.shape, 1)
    first_half = (lane % HEAD_DIM) < (HEAD_DIM // 2)
    cos_ref[...] = jnp.cos(ang)
    sin_ref[...] = jnp.where(first_half, -jnp.sin(ang), jnp.sin(ang))


def _rope_tables(positions):
    t = positions.size
    tm = min(t, 2048)
    inv_freq = ROPE_THETA ** (-jnp.arange(0, HEAD_DIM, 2, dtype=F32) / HEAD_DIM)
    invf = jnp.tile(inv_freq, LANES // (HEAD_DIM // 2)).reshape(1, LANES)
    return pl.pallas_call(
        _rope_table_kernel,
        out_shape=[jax.ShapeDtypeStruct((t, LANES), F32)] * 2,
        grid=(t // tm,),
        in_specs=[pl.BlockSpec((tm, 1), lambda i: (i, 0)),
                  pl.BlockSpec((1, LANES), lambda i: (0, 0))],
        out_specs=[pl.BlockSpec((tm, LANES), lambda i: (i, 0))] * 2,
        compiler_params=_params("parallel"),
        name="rope_tables",
    )(positions.reshape(t, 1), invf)


def _rope(x, cos, sin, first_half):
    half = HEAD_DIM // 2
    partner = jnp.where(first_half, pltpu.roll(x, LANES - half, axis=1), pltpu.roll(x, half, axis=1))
    return x * cos + partner * sin


def _inproj_kernel(x_ref, g_ref, w_ref, w2_ref, gb_ref, cos_ref, sin_ref,
                   qa_ref, ka_ref, va_ref, qg_ref, kg_ref, vg_ref, rg_ref, la_ref):
    hn = _rms(x_ref[...], g_ref[...]).astype(BF16)
    cos = cos_ref[...]
    sin = sin_ref[...]
    lane = lax.broadcasted_iota(jnp.int32, cos.shape, 1)
    first_half = (lane % HEAD_DIM) < (HEAD_DIM // 2)

    def proj(c0, width):
        return jnp.dot(hn, w_ref[:, c0:c0 + width], preferred_element_type=F32)

    def dup_into(dst_ref, s, slab):
        lo = lane < HEAD_DIM
        swapped = pltpu.roll(slab, HEAD_DIM, axis=1)
        dst_ref[:, (2 * s) * LANES:(2 * s + 1) * LANES] = jnp.where(lo, slab, swapped).astype(BF16)
        dst_ref[:, (2 * s + 1) * LANES:(2 * s + 2) * LANES] = jnp.where(lo, swapped, slab).astype(BF16)

    c = 0
    qa = proj(c, ATTN_WIDTH)
    for s in range(ATTN_WIDTH // LANES):
        cols = slice(s * LANES, (s + 1) * LANES)
        qa_ref[:, cols] = (_rope(qa[:, cols], cos, sin, first_half) * HEAD_DIM ** -0.5).astype(BF16)
    c += ATTN_WIDTH
    kv = proj(c, 2 * KV_WIDTH)
    for s in range(KV_WIDTH // LANES):
        dup_into(ka_ref, s, _rope(kv[:, s * LANES:(s + 1) * LANES], cos, sin, first_half))
        dup_into(va_ref, s, kv[:, KV_WIDTH + s * LANES:KV_WIDTH + (s + 1) * LANES])
    c += 2 * KV_WIDTH
    qg_ref[...] = proj(c, GLA_K_WIDTH)
    c += GLA_K_WIDTH
    kg_ref[...] = proj(c, GLA_K_WIDTH)
    c += GLA_K_WIDTH
    vg_ref[...] = proj(c, GLA_V_WIDTH).astype(BF16)
    c += GLA_V_WIDTH
    rg_ref[...] = proj(c, GLA_V_WIDTH).astype(BF16)
    c += GLA_V_WIDTH

    g_lr = proj(c, w_ref.shape[1] - c).astype(BF16)
    z = jnp.dot(g_lr, w2_ref[...], preferred_element_type=F32) + gb_ref[...]
    la_ref[...] = jax.nn.log_sigmoid(z) / GLA_TAU


def _inproj(x, gain, w_mix, w2, gate_b, cos, sin, layer):
    t, d = x.shape
    tm = PROJ_TM
    row = lambda width: pl.BlockSpec((tm, width), lambda i: (i, 0))
    widths = (ATTN_WIDTH, 2 * KV_WIDTH, 2 * KV_WIDTH, GLA_K_WIDTH, GLA_K_WIDTH, GLA_V_WIDTH,
              GLA_V_WIDTH, GLA_K_WIDTH)
    dtypes = (BF16, BF16, BF16, F32, F32, BF16, BF16, F32)
    return pl.pallas_call(
        _inproj_kernel,
        out_shape=[jax.ShapeDtypeStruct((t, w), dt) for w, dt in zip(widths, dtypes)],
        grid=(t // tm,),
        in_specs=[row(d), _resident((1, d)), _resident(w_mix.shape[1:], layer),
                  _resident(w2.shape[1:], layer), _resident((1, GLA_K_WIDTH)),
                  row(LANES), row(LANES)],
        out_specs=[row(w) for w in widths],
        compiler_params=_params("parallel"),
        name="mixer_inproj",
    )(x, gain, w_mix, w2, gate_b, cos, sin)


def _attn_kernel(sinks_ref, q_ref, kc_ref, vc_ref, kp_ref, vp_ref, o_ref):
    n = pl.program_id(1)
    blk = ATTN_BLOCK
    k_all = jnp.concatenate([kp_ref[...], kc_ref[...]], axis=0)
    v_all = jnp.concatenate([vp_ref[...], vc_ref[...]], axis=0)
    qi = lax.broadcasted_iota(jnp.int32, (2 * blk, blk), 0) % blk
    kj = lax.broadcasted_iota(jnp.int32, (2 * blk, blk), 1)
    top = lax.broadcasted_iota(jnp.int32, (2 * blk, 1), 0) < blk
    band_prev = kj > qi
    band_cur = kj <= qi
    band_prev_first = band_prev & (kj >= jnp.where(n > 0, 0, blk))
    lane = lax.broadcasted_iota(jnp.int32, (2 * blk, LANES), 1)
    key = lax.broadcasted_iota(jnp.int32, (2 * blk, LANES), 0)
    half = (lane < HEAD_DIM, lane >= HEAD_DIM)
    ones_half = tuple(jnp.where(hm, 1.0, 0.0).astype(BF16) for hm in half)
    zero = jnp.zeros((2 * blk, LANES), BF16)
    neg_inf = jnp.full((2 * blk, blk), -jnp.inf, F32)
    for g in range(N_KV_HEADS):
        gl = slice(g * LANES, (g + 1) * LANES)
        fills = []
        for parity in range(2):
            sink = jnp.where(top, sinks_ref[Q_PER_KV * g + parity], sinks_ref[Q_PER_KV * g + 2 + parity])
            fills.append(jnp.where(kj == 0, sink, neg_inf))
        for b in range(q_ref.shape[0] // blk):
            rows = slice(b * blk, (b + 1) * blk)
            q = jnp.concatenate([q_ref[rows, (2 * g) * LANES:(2 * g + 1) * LANES],
                                 q_ref[rows, (2 * g + 1) * LANES:(2 * g + 2) * LANES]], axis=0)
            kd = k_all[b * blk:(b + 2) * blk, gl]
            vd = v_all[b * blk:(b + 2) * blk, gl]
            mask_prev = band_prev_first if b == 0 else band_prev
            acc = None
            for parity in range(2):
                kz = jnp.where(half[parity], kd, zero)
                s = lax.dot_general(q, kz, (((1,), (1,)), ((), ())), preferred_element_type=F32)
                s_prev = jnp.where(mask_prev, s[:, :blk], fills[parity])
                s_cur = jnp.where(band_cur, s[:, blk:], neg_inf)
                m = jnp.max(jnp.maximum(s_prev, s_cur), axis=-1, keepdims=True)
                p = jnp.concatenate([jnp.exp(s_prev - m), jnp.exp(s_cur - m)], axis=1).astype(BF16)
                vz = jnp.where(half[parity] & (key > 0), vd, zero)
                w = jnp.concatenate([vz, ones_half[parity]], axis=1)
                part = jnp.dot(p, w, preferred_element_type=F32)
                acc = part if acc is None else acc + part
            out = (acc[:, :LANES] / acc[:, LANES:]).astype(BF16)
            o_ref[rows, (2 * g) * LANES:(2 * g + 1) * LANES] = out[:blk]
            o_ref[rows, (2 * g + 1) * LANES:(2 * g + 2) * LANES] = out[blk:]


def _attention(q, k, v, sinks, batch):
    t = q.shape[0]
    seq = t // batch
    tq = min(ATTN_TQ, seq)
    nq = seq // tq
    per = tq // ATTN_BLOCK
    kvw = k.shape[1]
    cur = lambda width: pl.BlockSpec((tq, width), lambda b, n: (b * nq + n, 0))
    prev = lambda width: pl.BlockSpec(
        (ATTN_BLOCK, width), lambda b, n: (jnp.maximum((b * nq + n) * per - 1, 0), 0))
    return pl.pallas_call(
        _attn_kernel,
        out_shape=jax.ShapeDtypeStruct((t, ATTN_WIDTH), BF16),
        grid=(batch, nq),
        in_specs=[pl.BlockSpec(memory_space=pltpu.SMEM),
                  cur(ATTN_WIDTH), cur(kvw), cur(kvw), prev(kvw), prev(kvw)],
        out_specs=cur(ATTN_WIDTH),
        compiler_params=_params("parallel", "arbitrary"),
        name="swa_attention",
    )(sinks, q, k, v, k, v)


def _gla_kernel(q_ref, k_ref, v_ref, la_ref, r_ref, gain_ref, o_ref, state_ref):
    @pl.when(pl.program_id(1) == 0)
    def _():
        state_ref[...] = jnp.zeros_like(state_ref)

    c = GLA_CHUNK
    ti = lax.broadcasted_iota(jnp.int32, (c, c), 0)
    si = lax.broadcasted_iota(jnp.int32, (c, c), 1)
    causal = ti >= si
    tri = causal.astype(F32)
    gain = gain_ref[...]
    scale = GLA_DK ** -0.5
    nt = (((1,), (1,)), ((), ()))
    state_t = [state_ref[h] for h in range(GLA_HEADS)]
    for ci in range(q_ref.shape[0] // c):
        rows = slice(ci * c, (ci + 1) * c)
        la = la_ref[rows, :]
        b = jnp.dot(tri, la, precision=lax.Precision.HIGHEST, preferred_element_type=F32)
        b_last = b[c - 1:c, :]
        q_dec = (q_ref[rows, :] * scale) * jnp.exp(b)
        k_all = k_ref[rows, :]
        k_dec = k_all * jnp.exp(-b)
        k_rem = k_all * jnp.exp(b_last - b)
        chunk_decay = jnp.exp(b_last)
        for h in range(GLA_HEADS):
            kl = slice(h * GLA_DK, (h + 1) * GLA_DK)
            vl = slice(h * GLA_DV, (h + 1) * GLA_DV)
            qh = q_dec[:, kl].astype(BF16)
            v_t = v_ref[rows, vl].astype(F32).T.astype(BF16)
            intra = lax.dot_general(qh, k_dec[:, kl].astype(BF16), nt, preferred_element_type=F32)
            intra = jnp.where(causal, intra, 0.0).astype(BF16)
            o = lax.dot_general(jnp.concatenate([qh, intra], axis=1),
                                jnp.concatenate([state_t[h].astype(BF16), v_t], axis=1),
                                nt, preferred_element_type=F32)
            d_state_t = jnp.dot(v_t, k_rem[:, kl].astype(BF16), preferred_element_type=F32)
            state_t[h] = state_t[h] * chunk_decay[:, kl] + d_state_t
            r = r_ref[rows, vl].astype(F32)
            o_ref[rows, vl] = (_rms(o, gain) * _silu(r)).astype(BF16)
    for h in range(GLA_HEADS):
        state_ref[h] = state_t[h]


def _gla(q, k, v, log_a, r, gain, batch):
    t = q.shape[0]
    seq = t // batch
    ts = min(GLA_TS, seq)
    ns = seq // ts
    row = lambda width: pl.BlockSpec((ts, width), lambda b, n: (b * ns + n, 0))
    return pl.pallas_call(
        _gla_kernel,
        out_shape=jax.ShapeDtypeStruct((t, GLA_V_WIDTH), BF16),
        grid=(batch, ns),
        in_specs=[row(GLA_K_WIDTH), row(GLA_K_WIDTH), row(GLA_V_WIDTH), row(GLA_K_WIDTH),
                  row(GLA_V_WIDTH), pl.BlockSpec((1, GLA_DV), lambda b, n: (0, 0))],
        out_specs=row(GLA_V_WIDTH),
        scratch_shapes=[pltpu.VMEM((GLA_HEADS, GLA_DV, GLA_DK), F32)],
        compiler_params=_params("parallel", "arbitrary"),
        name="gla",
    )(q, k, v, log_a, r, gain)


def _outproj_kernel(x_ref, a_ref, g_ref, w_ref, gain_ref, o_ref):
    h = (jnp.dot(a_ref[...], w_ref[:ATTN_WIDTH, :], preferred_element_type=F32)
         + jnp.dot(g_ref[...], w_ref[ATTN_WIDTH:, :], preferred_element_type=F32))
    o_ref[...] = x_ref[...] + _rms(h, gain_ref[...])


def _outproj(x, attn, gla, w_out, gain, layer):
    t, d = x.shape
    tm = PROJ_TM
    row = lambda width: pl.BlockSpec((tm, width), lambda i: (i, 0))
    return pl.pallas_call(
        _outproj_kernel,
        out_shape=jax.ShapeDtypeStruct((t, d), F32),
        grid=(t // tm,),
        in_specs=[row(d), row(ATTN_WIDTH), row(GLA_V_WIDTH), _resident(w_out.shape[1:], layer),
                  _resident((1, d))],
        out_specs=row(d),
        compiler_params=_params("parallel"),
        name="mixer_outproj",
    )(x, attn, gla, w_out, gain)


def _pad_to(a, axis, multiple):
    pad = (-a.shape[axis]) % multiple
    widths = [(0, 0)] * a.ndim
    widths[axis] = (0, pad)
    return jnp.pad(a, widths)


def _prep_ffn_weights(ffn_w_in, ffn_w_out):
    wg, wu = _cast_w_in(ffn_w_in)
    return wg, wu, _pad_to(ffn_w_out, -2, FFN_TF).astype(BF16)


def _prep_mixer_weights(w_mix_in, gla_gate_w2, w_mix_out):
    w_mix = _pad_to(w_mix_in, -1, 2 * LANES).astype(BF16)
    main = w_mix_in.shape[-1] - GLA_GATE_RANK
    w2 = jnp.pad(gla_gate_w2, ((0, 0), (0, w_mix.shape[-1] - main - GLA_GATE_RANK), (0, 0))).astype(BF16)
    return w_mix, w2, w_mix_out.astype(BF16)


def kernel(x, positions, norm_gains, ffn_w_in, ffn_w_out, w_mix_in, attn_sinks, gla_gate_w2,
           gla_gate_b, gla_norm_gain, w_mix_out):
    batch, seq, d = x.shape
    depth = norm_gains.shape[0]
    xt = x.reshape(batch * seq, d)
    cos, sin = _rope_tables(positions)
    w_ffn = _prep_ffn_weights(ffn_w_in, ffn_w_out)
    w_mix, w2, w_mo = _prep_mixer_weights(w_mix_in, gla_gate_w2, w_mix_out)
    for l in range(depth):
        g = norm_gains[l].reshape(-1, 1, d)
        xt = _ffn(xt, g[0], g[1], w_ffn, l, 0)
        qa, ka, va, qg, kg, vg, rg, la = _inproj(
            xt, g[2], w_mix, w2, gla_gate_b[l].reshape(1, -1), cos, sin, l)
        attn = _attention(qa, ka, va, attn_sinks[l], batch)
        gla = _gla(qg, kg, vg, la, rg, gla_norm_gain[l].reshape(1, -1), batch)
        xt = _outproj(xt, attn, gla, w_mo, g[3], l)
        xt = _ffn(xt, g[4], g[5], w_ffn, l, 1)
    return xt.reshape(batch, seq, d)
```

```python
import functools

import numpy as np
import jax
import jax.numpy as jnp
from jax import lax
from jax.experimental import pallas as pl
from jax.experimental.pallas import tpu as pltpu

F32 = jnp.float32
BF16 = jnp.bfloat16

HEAD_DIM = 64
N_Q_HEADS = 16
N_KV_HEADS = 4
Q_PER_KV = N_Q_HEADS // N_KV_HEADS
ATTN_WIDTH = N_Q_HEADS * HEAD_DIM
KV_WIDTH = N_KV_HEADS * HEAD_DIM
ATTN_BLOCK = 128
ROPE_THETA = 10000.0
GLA_HEADS = 4
GLA_DK = 128
GLA_DV = 256
GLA_K_WIDTH = GLA_HEADS * GLA_DK
GLA_V_WIDTH = GLA_HEADS * GLA_DV
GLA_GATE_RANK = 16
GLA_TAU = 16.0
GLA_CHUNK = 64
NORM_EPS = 1e-6

LANES = 128
VMEM_LIMIT_BYTES = 60000 * 1024

FFN_TM = 1024
FFN_TF = 512
FFN_LAST_PARTS = 4
PROJ_TM = 512
ATTN_TQ = 256
GLA_TS = 256
CAST_ROWS = 256


def _params(*sem):
    return pltpu.CompilerParams(dimension_semantics=sem, vmem_limit_bytes=VMEM_LIMIT_BYTES)


def _resident(shape, lead=()):
    lead = tuple(lead) if isinstance(lead, (tuple, list)) else (lead,)
    return pl.BlockSpec((None,) * len(lead) + tuple(shape), lambda *_: lead + (0,) * len(shape),
                        pipeline_mode=pl.Buffered(1))


def _rms(x, gain):
    ms = jnp.mean(x * x, axis=-1, keepdims=True)
    return (x * lax.rsqrt(ms + NORM_EPS)) * gain


def _silu(x):
    return x * jax.nn.sigmoid(x)


def _swiglu_chunk(hn, wg, wu, wo):
    gate = jnp.dot(hn, wg, preferred_element_type=F32)
    up = jnp.dot(hn, wu, preferred_element_type=F32)
    act = (_silu(gate) * up).astype(BF16)
    return jnp.dot(act, wo, preferred_element_type=F32)


def _ffn_kernel(x_ref, gin_ref, gout_ref, wg_ref, wu_ref, wo_ref, o_ref, hn_ref):
    j = pl.program_id(1)
    last = pl.num_programs(1) - 1

    @pl.when(j == 0)
    def _():
        hn = _rms(x_ref[...], gin_ref[...]).astype(BF16)
        hn_ref[...] = hn
        o_ref[...] = _swiglu_chunk(hn, wg_ref[...], wu_ref[...], wo_ref[...])

    @pl.when((j > 0) & (j < last))
    def _():
        o_ref[...] += _swiglu_chunk(hn_ref[...], wg_ref[...], wu_ref[...], wo_ref[...])

    @pl.when(j == last)
    def _():
        part = o_ref.shape[0] // FFN_LAST_PARTS
        for r in range(FFN_LAST_PARTS):
            rows = slice(r * part, (r + 1) * part)
            h = o_ref[rows, :] + _swiglu_chunk(hn_ref[rows, :], wg_ref[...], wu_ref[...], wo_ref[...])
            o_ref[rows, :] = x_ref[rows, :] + 0.5 * _rms(h, gout_ref[...])


def _ffn(x, gin, gout, w, layer, slot):
    wg, wu, wo = w
    t, d = x.shape
    nc = wg.shape[2]
    assert nc >= 2 and wo.shape[2] == nc * FFN_TF
    chunk = pl.BlockSpec((None, None, None, d, FFN_TF), lambda i, j: (layer, slot, j, 0, 0))
    return pl.pallas_call(
        _ffn_kernel,
        out_shape=jax.ShapeDtypeStruct((t, d), F32),
        grid=(t // FFN_TM, nc),
        in_specs=[
            pl.BlockSpec((FFN_TM, d), lambda i, j: (i, 0)),
            pl.BlockSpec((1, d), lambda i, j: (0, 0)),
            pl.BlockSpec((1, d), lambda i, j: (0, 0)),
            chunk,
            chunk,
            pl.BlockSpec((None, None, FFN_TF, d), lambda i, j: (layer, slot, j, 0)),
        ],
        out_specs=pl.BlockSpec((FFN_TM, d), lambda i, j: (i, 0)),
        scratch_shapes=[pltpu.VMEM((FFN_TM, d), BF16)],
        compiler_params=_params("parallel", "arbitrary"),
        name="ffn",
    )(x, gin, gout, wg, wu, wo)


def _cast_w_in_kernel(w_ref, g_ref, u_ref):
    d_ff = w_ref.shape[1] // 2
    for half, dst in ((0, g_ref), (1, u_ref)):
        for c in range(dst.shape[0]):
            lo = c * FFN_TF
            width = min(FFN_TF, d_ff - lo)
            dst[c, :, :width] = w_ref[:, half * d_ff + lo:half * d_ff + lo + width].astype(BF16)
            if width < FFN_TF:
                dst[c, :, width:] = jnp.zeros((dst.shape[1], FFN_TF - width), BF16)


def _cast_w_in(ffn_w_in):
    depth, two, d, f2 = ffn_w_in.shape
    nc = pl.cdiv(f2 // 2, FFN_TF)
    rt = CAST_ROWS
    out = jax.ShapeDtypeStruct((depth, two, nc, d, FFN_TF), BF16)
    out_spec = pl.BlockSpec((None, None, nc, rt, FFN_TF), lambda m, i: (m // two, m % two, 0, i, 0))
    return pl.pallas_call(
        _cast_w_in_kernel,
        out_shape=[out, out],
        grid=(depth * two, d // rt),
        in_specs=[pl.BlockSpec((None, None, rt, f2), lambda m, i: (m // two, m % two, i, 0))],
        out_specs=[out_spec, out_spec],
        compiler_params=_params("parallel", "parallel"),
        name="cast_ffn_w_in",
    )(ffn_w_in)


def _rope_table_kernel(pos_ref, invf_ref, cos_ref, sin_ref):
    ang = pos_ref[...].astype(F32) * invf_ref[...]
    lane = lax.broadcasted_iota(jnp.int32, ang.shape, 1)
    first_half = (lane % HEAD_DIM) < (HEAD_DIM // 2)
    cos_ref[...] = jnp.cos(ang)
    sin_ref[...] = jnp.where(first_half, -jnp.sin(ang), jnp.sin(ang))


def _rope_tables(positions):
    t = positions.size
    tm = min(t, 2048)
    inv_freq = ROPE_THETA ** (-jnp.arange(0, HEAD_DIM, 2, dtype=F32) / HEAD_DIM)
    invf = jnp.tile(inv_freq, LANES // (HEAD_DIM // 2)).reshape(1, LANES)
    return pl.pallas_call(
        _rope_table_kernel,
        out_shape=[jax.ShapeDtypeStruct((t, LANES), F32)] * 2,
        grid=(t // tm,),
        in_specs=[pl.BlockSpec((tm, 1), lambda i: (i, 0)),
                  pl.BlockSpec((1, LANES), lambda i: (0, 0))],
        out_specs=[pl.BlockSpec((tm, LANES), lambda i: (i, 0))] * 2,
        compiler_params=_params("parallel"),
        name="rope_tables",
    )(positions.reshape(t, 1), invf)


def _rope(x, cos, sin, first_half):
    half = HEAD_DIM // 2
    partner = jnp.where(first_half, pltpu.roll(x, LANES - half, axis=1), pltpu.roll(x, half, axis=1))
    return x * cos + partner * sin


def _inproj_kernel(x_ref, g_ref, w_ref, w2_ref, gb_ref, cos_ref, sin_ref,
                   qa_ref, ka_ref, va_ref, qg_ref, kg_ref, vg_ref, rg_ref, la_ref):
    hn = _rms(x_ref[...], g_ref[...]).astype(BF16)
    cos = cos_ref[...]
    sin = sin_ref[...]
    lane = lax.broadcasted_iota(jnp.int32, cos.shape, 1)
    first_half = (lane % HEAD_DIM) < (HEAD_DIM // 2)

    def proj(c0, width):
        return jnp.dot(hn, w_ref[:, c0:c0 + width], preferred_element_type=F32)

    def dup_into(dst_ref, s, slab):
        lo = lane < HEAD_DIM
        swapped = pltpu.roll(slab, HEAD_DIM, axis=1)
        dst_ref[:, (2 * s) * LANES:(2 * s + 1) * LANES] = jnp.where(lo, slab, swapped).astype(BF16)
        dst_ref[:, (2 * s + 1) * LANES:(2 * s + 2) * LANES] = jnp.where(lo, swapped, slab).astype(BF16)

    c = 0
    qa = proj(c, ATTN_WIDTH)
    for s in range(ATTN_WIDTH // LANES):
        cols = slice(s * LANES, (s + 1) * LANES)
        qa_ref[:, cols] = (_rope(qa[:, cols], cos, sin, first_half) * HEAD_DIM ** -0.5).astype(BF16)
    c += ATTN_WIDTH
    kv = proj(c, 2 * KV_WIDTH)
    for s in range(KV_WIDTH // LANES):
        dup_into(ka_ref, s, _rope(kv[:, s * LANES:(s + 1) * LANES], cos, sin, first_half))
        dup_into(va_ref, s, kv[:, KV_WIDTH + s * LANES:KV_WIDTH + (s + 1) * LANES])
    c += 2 * KV_WIDTH
    qg_ref[...] = proj(c, GLA_K_WIDTH)
    c += GLA_K_WIDTH
    kg_ref[...] = proj(c, GLA_K_WIDTH)
    c += GLA_K_WIDTH
    vg_ref[...] = proj(c, GLA_V_WIDTH).astype(BF16)
    c += GLA_V_WIDTH
    rg_ref[...] = proj(c, GLA_V_WIDTH).astype(BF16)
    c += GLA_V_WIDTH

    g_lr = proj(c, w_ref.shape[1] - c).astype(BF16)
    z = jnp.dot(g_lr, w2_ref[...], preferred_element_type=F32) + gb_ref[...]
    la_ref[...] = jax.nn.log_sigmoid(z) / GLA_TAU


def _inproj(x, gain, w_mix, w2, gate_b, cos, sin, layer):
    t, d = x.shape
    tm = PROJ_TM
    row = lambda width: pl.BlockSpec((tm, width), lambda i: (i, 0))
    widths = (ATTN_WIDTH, 2 * KV_WIDTH, 2 * KV_WIDTH, GLA_K_WIDTH, GLA_K_WIDTH, GLA_V_WIDTH,
              GLA_V_WIDTH, GLA_K_WIDTH)
    dtypes = (BF16, BF16, BF16, F32, F32, BF16, BF16, F32)
    return pl.pallas_call(
        _inproj_kernel,
        out_shape=[jax.ShapeDtypeStruct((t, w), dt) for w, dt in zip(widths, dtypes)],
        grid=(t // tm,),
        in_specs=[row(d), _resident((1, d)), _resident(w_mix.shape[1:], layer),
                  _resident(w2.shape[1:], layer), _resident((1, GLA_K_WIDTH)),
                  row(LANES), row(LANES)],
        out_specs=[row(w) for w in widths],
        compiler_params=_params("parallel"),
        name="mixer_inproj",
    )(x, gain, w_mix, w2, gate_b, cos, sin)


def _attn_kernel(sinks_ref, q_ref, kc_ref, vc_ref, kp_ref, vp_ref, o_ref):
    n = pl.program_id(1)
    blk = ATTN_BLOCK
    k_all = jnp.concatenate([kp_ref[...], kc_ref[...]], axis=0)
    v_all = jnp.concatenate([vp_ref[...], vc_ref[...]], axis=0)
    qi = lax.broadcasted_iota(jnp.int32, (2 * blk, blk), 0) % blk
    kj = lax.broadcasted_iota(jnp.int32, (2 * blk, blk), 1)
    top = lax.broadcasted_iota(jnp.int32, (2 * blk, 1), 0) < blk
    band_prev = kj > qi
    band_cur = kj <= qi
    band_prev_first = band_prev & (kj >= jnp.where(n > 0, 0, blk))
    lane = lax.broadcasted_iota(jnp.int32, (2 * blk, LANES), 1)
    key = lax.broadcasted_iota(jnp.int32, (2 * blk, LANES), 0)
    half = (lane < HEAD_DIM, lane >= HEAD_DIM)
    ones_half = tuple(jnp.where(hm, 1.0, 0.0).astype(BF16) for hm in half)
    zero = jnp.zeros((2 * blk, LANES), BF16)
    neg_inf = jnp.full((2 * blk, blk), -jnp.inf, F32)
    for g in range(N_KV_HEADS):
        gl = slice(g * LANES, (g + 1) * LANES)
        fills = []
        for parity in range(2):
            sink = jnp.where(top, sinks_ref[Q_PER_KV * g + parity], sinks_ref[Q_PER_KV * g + 2 + parity])
            fills.append(jnp.where(kj == 0, sink, neg_inf))
        for b in range(q_ref.shape[0] // blk):
            rows = slice(b * blk, (b + 1) * blk)
            q = jnp.concatenate([q_ref[rows, (2 * g) * LANES:(2 * g + 1) * LANES],
                                 q_ref[rows, (2 * g + 1) * LANES:(2 * g + 2) * LANES]], axis=0)
            kd = k_all[b * blk:(b + 2) * blk, gl]
            vd = v_all[b * blk:(b + 2) * blk, gl]
            mask_prev = band_prev_first if b == 0 else band_prev
            acc = None
            for parity in range(2):
                kz = jnp.where(half[parity], kd, zero)
                s = lax.dot_general(q, kz, (((1,), (1,)), ((), ())), preferred_element_type=F32)
                s_prev = jnp.where(mask_prev, s[:, :blk], fills[parity])
                s_cur = jnp.where(band_cur, s[:, blk:], neg_inf)
                m = jnp.max(jnp.maximum(s_prev, s_cur), axis=-1, keepdims=True)
                p = jnp.concatenate([jnp.exp(s_prev - m), jnp.exp(s_cur - m)], axis=1).astype(BF16)
                vz = jnp.where(half[parity] & (key > 0), vd, zero)
                w = jnp.concatenate([vz, ones_half[parity]], axis=1)
                part = jnp.dot(p, w, preferred_element_type=F32)
                acc = part if acc is None else acc + part
            out = (acc[:, :LANES] / acc[:, LANES:]).astype(BF16)
            o_ref[rows, (2 * g) * LANES:(2 * g + 1) * LANES] = out[:blk]
            o_ref[rows, (2 * g + 1) * LANES:(2 * g + 2) * LANES] = out[blk:]


def _attention(q, k, v, sinks, batch):
    t = q.shape[0]
    seq = t // batch
    tq = min(ATTN_TQ, seq)
    nq = seq // tq
    per = tq // ATTN_BLOCK
    kvw = k.shape[1]
    cur = lambda width: pl.BlockSpec((tq, width), lambda b, n: (b * nq + n, 0))
    prev = lambda width: pl.BlockSpec(
        (ATTN_BLOCK, width), lambda b, n: (jnp.maximum((b * nq + n) * per - 1, 0), 0))
    return pl.pallas_call(
        _attn_kernel,
        out_shape=jax.ShapeDtypeStruct((t, ATTN_WIDTH), BF16),
        grid=(batch, nq),
        in_specs=[pl.BlockSpec(memory_space=pltpu.SMEM),
                  cur(ATTN_WIDTH), cur(kvw), cur(kvw), prev(kvw), prev(kvw)],
        out_specs=cur(ATTN_WIDTH),
        compiler_params=_params("parallel", "arbitrary"),
        name="swa_attention",
    )(sinks, q, k, v, k, v)


def _gla_kernel(q_ref, k_ref, v_ref, la_ref, r_ref, gain_ref, o_ref, state_ref):
    @pl.when(pl.program_id(1) == 0)
    def _():
        state_ref[...] = jnp.zeros_like(state_ref)

    c = GLA_CHUNK
    ti = lax.broadcasted_iota(jnp.int32, (c, c), 0)
    si = lax.broadcasted_iota(jnp.int32, (c, c), 1)
    causal = ti >= si
    tri = causal.astype(F32)
    gain = gain_ref[...]
    scale = GLA_DK ** -0.5
    nt = (((1,), (1,)), ((), ()))
    state_t = [state_ref[h] for h in range(GLA_HEADS)]
    for ci in range(q_ref.shape[0] // c):
        rows = slice(ci * c, (ci + 1) * c)
        la = la_ref[rows, :]
        b = jnp.dot(tri, la, precision=lax.Precision.HIGHEST, preferred_element_type=F32)
        b_last = b[c - 1:c, :]
        q_dec = (q_ref[rows, :] * scale) * jnp.exp(b)
        k_all = k_ref[rows, :]
        k_dec = k_all * jnp.exp(-b)
        k_rem = k_all * jnp.exp(b_last - b)
        chunk_decay = jnp.exp(b_last)
        for h in range(GLA_HEADS):
            kl = slice(h * GLA_DK, (h + 1) * GLA_DK)
            vl = slice(h * GLA_DV, (h + 1) * GLA_DV)
            qh = q_dec[:, kl].astype(BF16)
            v_t = v_ref[rows, vl].astype(F32).T.astype(BF16)
            intra = lax.dot_general(qh, k_dec[:, kl].astype(BF16), nt, preferred_element_type=F32)
            intra = jnp.where(causal, intra, 0.0).astype(BF16)
            o = lax.dot_general(jnp.concatenate([qh, intra], axis=1),
                                jnp.concatenate([state_t[h].astype(BF16), v_t], axis=1),
                                nt, preferred_element_type=F32)
            d_state_t = jnp.dot(v_t, k_rem[:, kl].astype(BF16), preferred_element_type=F32)
            state_t[h] = state_t[h] * chunk_decay[:, kl] + d_state_t
            r = r_ref[rows, vl].astype(F32)
            o_ref[rows, vl] = (_rms(o, gain) * _silu(r)).astype(BF16)
    for h in range(GLA_HEADS):
        state_ref[h] = state_t[h]


def _gla(q, k, v, log_a, r, gain, batch):
    t = q.shape[0]
    seq = t // batch
    ts = min(GLA_TS, seq)
    ns = seq // ts
    row = lambda width: pl.BlockSpec((ts, width), lambda b, n: (b * ns + n, 0))
    return pl.pallas_call(
        _gla_kernel,
        out_shape=jax.ShapeDtypeStruct((t, GLA_V_WIDTH), BF16),
        grid=(batch, ns),
        in_specs=[row(GLA_K_WIDTH), row(GLA_K_WIDTH), row(GLA_V_WIDTH), row(GLA_K_WIDTH),
                  row(GLA_V_WIDTH), pl.BlockSpec((1, GLA_DV), lambda b, n: (0, 0))],
        out_specs=row(GLA_V_WIDTH),
        scratch_shapes=[pltpu.VMEM((GLA_HEADS, GLA_DV, GLA_DK), F32)],
        compiler_params=_params("parallel", "arbitrary"),
        name="gla",
    )(q, k, v, log_a, r, gain)


def _outproj_kernel(x_ref, a_ref, g_ref, w_ref, gain_ref, o_ref):
    h = (jnp.dot(a_ref[...], w_ref[:ATTN_WIDTH, :], preferred_element_type=F32)
         + jnp.dot(g_ref[...], w_ref[ATTN_WIDTH:, :], preferred_element_type=F32))
    o_ref[...] = x_ref[...] + _rms(h, gain_ref[...])


def _outproj(x, attn, gla, w_out, gain, layer):
    t, d = x.shape
    tm = PROJ_TM
    row = lambda width: pl.BlockSpec((tm, width), lambda i: (i, 0))
    return pl.pallas_call(
        _outproj_kernel,
        out_shape=jax.ShapeDtypeStruct((t, d), F32),
        grid=(t // tm,),
        in_specs=[row(d), row(ATTN_WIDTH), row(GLA_V_WIDTH), _resident(w_out.shape[1:], layer),
                  _resident((1, d))],
        out_specs=row(d),
        compiler_params=_params("parallel"),
        name="mixer_outproj",
    )(x, attn, gla, w_out, gain)


def _pad_to(a, axis, multiple):
    pad = (-a.shape[axis]) % multiple
    widths = [(0, 0)] * a.ndim
    widths[axis] = (0, pad)
    return jnp.pad(a, widths)


def _prep_ffn_weights(ffn_w_in, ffn_w_out):
    wg, wu = _cast_w_in(ffn_w_in)
    return wg, wu, _pad_to(ffn_w_out, -2, FFN_TF).astype(BF16)


def _prep_mixer_weights(w_mix_in, gla_gate_w2, w_mix_out):
    w_mix = _pad_to(w_mix_in, -1, 2 * LANES).astype(BF16)
    main = w_mix_in.shape[-1] - GLA_GATE_RANK
    w2 = jnp.pad(gla_gate_w2, ((0, 0), (0, w_mix.shape[-1] - main - GLA_GATE_RANK), (0, 0))).astype(BF16)
    return w_mix, w2, w_mix_out.astype(BF16)


def kernel(x, positions, norm_gains, ffn_w_in, ffn_w_out, w_mix_in, attn_sinks, gla_gate_w2,
           gla_gate_b, gla_norm_gain, w_mix_out):
    batch, seq, d = x.shape
    depth = norm_gains.shape[0]
    xt = x.reshape(batch * seq, d)
    cos, sin = _rope_tables(positions)
    w_ffn = _prep_ffn_weights(ffn_w_in, ffn_w_out)
    w_mix, w2, w_mo = _prep_mixer_weights(w_mix_in, gla_gate_w2, w_mix_out)
    for l in range(depth):
        g = norm_gains[l].reshape(-1, 1, d)
        xt = _ffn(xt, g[0], g[1], w_ffn, l, 0)
        qa, ka, va, qg, kg, vg, rg, la = _inproj(
            xt, g[2], w_mix, w2, gla_gate_b[l].reshape(1, -1), cos, sin, l)
        attn = _attention(qa, ka, va, attn_sinks[l], batch)
        gla = _gla(qg, kg, vg, la, rg, gla_norm_gain[l].reshape(1, -1), batch)
        xt = _outproj(xt, attn, gla, w_mo, g[3], l)
        xt = _ffn(xt, g[4], g[5], w_ffn, l, 1)
    return xt.reshape(batch, seq, d)
```

```python
import functools

import numpy as np
import jax
import jax.numpy as jnp
from jax import lax
from jax.experimental import pallas as pl
from jax.experimental.pallas import tpu as pltpu

F32 = jnp.float32
BF16 = jnp.bfloat16

HEAD_DIM = 64
N_Q_HEADS = 16
N_KV_HEADS = 4
Q_PER_KV = N_Q_HEADS // N_KV_HEADS
ATTN_WIDTH = N_Q_HEADS * HEAD_DIM
KV_WIDTH = N_KV_HEADS * HEAD_DIM
ATTN_BLOCK = 128
ROPE_THETA = 10000.0
GLA_HEADS = 4
GLA_DK = 128
GLA_DV = 256
GLA_K_WIDTH = GLA_HEADS * GLA_DK
GLA_V_WIDTH = GLA_HEADS * GLA_DV
GLA_GATE_RANK = 16
GLA_TAU = 16.0
GLA_CHUNK = 64
NORM_EPS = 1e-6

LANES = 128
VMEM_LIMIT_BYTES = 60000 * 1024

FFN_TM = 1024
FFN_TF = 512
FFN_LAST_PARTS = 4
PROJ_TM = 512
OUTPROJ_PARTS = 2
ATTN_TQ = 512
GLA_TS = 512
CAST_ROWS = 256


def _params(*sem):
    return pltpu.CompilerParams(dimension_semantics=sem, vmem_limit_bytes=VMEM_LIMIT_BYTES)


def _resident(shape, lead=()):
    lead = tuple(lead) if isinstance(lead, (tuple, list)) else (lead,)
    return pl.BlockSpec((None,) * len(lead) + tuple(shape), lambda *_: lead + (0,) * len(shape),
                        pipeline_mode=pl.Buffered(1))


def _rms(x, gain):
    ms = jnp.mean(x * x, axis=-1, keepdims=True)
    return (x * lax.rsqrt(ms + NORM_EPS)) * gain


def _silu(x):
    return x * jax.nn.sigmoid(x)


def _swiglu_chunk(hn, wgu, wo):
    gu = jnp.dot(hn, wgu, preferred_element_type=F32)
    act = (_silu(gu[:, :FFN_TF]) * gu[:, FFN_TF:]).astype(BF16)
    return jnp.dot(act, wo, preferred_element_type=F32)


def _ffn_kernel(x_ref, gin_ref, gout_ref, wgu_ref, wo_ref, wot_ref, o_ref, hn_ref):
    j = pl.program_id(1)
    last = pl.num_programs(1) - 1

    @pl.when(j == 0)
    def _():
        hn = _rms(x_ref[...], gin_ref[...]).astype(BF16)
        hn_ref[...] = hn
        o_ref[...] = _swiglu_chunk(hn, wgu_ref[...], wo_ref[...])

    @pl.when((j > 0) & (j < last))
    def _():
        o_ref[...] += _swiglu_chunk(hn_ref[...], wgu_ref[...], wo_ref[...])

    @pl.when(j == last)
    def _():
        part = o_ref.shape[0] // FFN_LAST_PARTS
        for r in range(FFN_LAST_PARTS):
            rows = slice(r * part, (r + 1) * part)
            h = o_ref[rows, :] + _swiglu_chunk(hn_ref[rows, :], wgu_ref[...], wot_ref[...])
            o_ref[rows, :] = x_ref[rows, :] + 0.5 * _rms(h, gout_ref[...])


def _ffn(x, gin, gout, w, layer, slot):
    wgu, wo, wot = w
    t, d = x.shape
    nc = wgu.shape[2]
    n_full = wo.shape[2] // FFN_TF
    assert nc >= 2 and n_full == nc - 1
    return pl.pallas_call(
        _ffn_kernel,
        out_shape=jax.ShapeDtypeStruct((t, d), F32),
        grid=(t // FFN_TM, nc),
        in_specs=[
            pl.BlockSpec((FFN_TM, d), lambda i, j: (i, 0)),
            pl.BlockSpec((1, d), lambda i, j: (0, 0)),
            pl.BlockSpec((1, d), lambda i, j: (0, 0)),
            pl.BlockSpec((None, None, None, d, 2 * FFN_TF), lambda i, j: (layer, slot, j, 0, 0)),
            pl.BlockSpec((None, None, FFN_TF, d),
                         lambda i, j: (layer, slot, jnp.minimum(j, n_full - 1), 0)),
            _resident((FFN_TF, d), (layer, slot)),
        ],
        out_specs=pl.BlockSpec((FFN_TM, d), lambda i, j: (i, 0)),
        scratch_shapes=[pltpu.VMEM((FFN_TM, d), BF16)],
        compiler_params=_params("parallel", "arbitrary"),
        name="ffn",
    )(x, gin, gout, wgu, wo, wot)


def _cast_w_in_kernel(w_ref, dst):
    d_ff = w_ref.shape[1] // 2
    for c in range(dst.shape[0]):
        lo = c * FFN_TF
        width = min(FFN_TF, d_ff - lo)
        for half in range(2):
            src = half * d_ff + lo
            dst[c, :, half * FFN_TF:half * FFN_TF + width] = w_ref[:, src:src + width].astype(BF16)
            if width < FFN_TF:
                dst[c, :, half * FFN_TF + width:(half + 1) * FFN_TF] = jnp.zeros(
                    (dst.shape[1], FFN_TF - width), BF16)


def _cast_w_in(ffn_w_in):
    depth, two, d, f2 = ffn_w_in.shape
    nc = pl.cdiv(f2 // 2, FFN_TF)
    rt = CAST_ROWS
    return pl.pallas_call(
        _cast_w_in_kernel,
        out_shape=jax.ShapeDtypeStruct((depth, two, nc, d, 2 * FFN_TF), BF16),
        grid=(depth * two, d // rt),
        in_specs=[pl.BlockSpec((None, None, rt, f2), lambda m, i: (m // two, m % two, i, 0))],
        out_specs=pl.BlockSpec((None, None, nc, rt, 2 * FFN_TF),
                               lambda m, i: (m // two, m % two, 0, i, 0)),
        compiler_params=_params("parallel", "parallel"),
        name="cast_ffn_w_in",
    )(ffn_w_in)


def _rope_table_kernel(pos_ref, invf_ref, cos_ref, sin_ref):
    ang = pos_ref[...].astype(F32) * invf_ref[...]
    lane = lax.broadcasted_iota(jnp.int32, ang.shape, 1)
    first_half = (lane % HEAD_DIM) < (HEAD_DIM // 2)
    cos_ref[...] = jnp.cos(ang)
    sin_ref[...] = jnp.where(first_half, -jnp.sin(ang), jnp.sin(ang))


def _rope_tables(positions):
    t = positions.size
    tm = min(t, 2048)
    inv_freq = ROPE_THETA ** (-jnp.arange(0, HEAD_DIM, 2, dtype=F32) / HEAD_DIM)
    invf = jnp.tile(inv_freq, LANES // (HEAD_DIM // 2)).reshape(1, LANES)
    return pl.pallas_call(
        _rope_table_kernel,
        out_shape=[jax.ShapeDtypeStruct((t, LANES), F32)] * 2,
        grid=(t // tm,),
        in_specs=[pl.BlockSpec((tm, 1), lambda i: (i, 0)),
                  pl.BlockSpec((1, LANES), lambda i: (0, 0))],
        out_specs=[pl.BlockSpec((tm, LANES), lambda i: (i, 0))] * 2,
        compiler_params=_params("parallel"),
        name="rope_tables",
    )(positions.reshape(t, 1), invf)


def _rope(x, cos, sin, first_half):
    half = HEAD_DIM // 2
    partner = jnp.where(first_half, pltpu.roll(x, LANES - half, axis=1), pltpu.roll(x, half, axis=1))
    return x * cos + partner * sin


def _inproj_kernel(x_ref, g_ref, w_ref, w2_ref, gb_ref, cos_ref, sin_ref,
                   qa_ref, ka_ref, va_ref, qg_ref, kg_ref, vg_ref, rg_ref, la_ref):
    hn = _rms(x_ref[...], g_ref[...]).astype(BF16)
    cos = cos_ref[...]
    sin = sin_ref[...]
    lane = lax.broadcasted_iota(jnp.int32, cos.shape, 1)
    first_half = (lane % HEAD_DIM) < (HEAD_DIM // 2)

    def proj(c0, width):
        return jnp.dot(hn, w_ref[:, c0:c0 + width], preferred_element_type=F32)

    def dup_into(dst_ref, s, slab):
        lo = lane < HEAD_DIM
        swapped = pltpu.roll(slab, HEAD_DIM, axis=1)
        dst_ref[:, (2 * s) * LANES:(2 * s + 1) * LANES] = jnp.where(lo, slab, swapped).astype(BF16)
        dst_ref[:, (2 * s + 1) * LANES:(2 * s + 2) * LANES] = jnp.where(lo, swapped, slab).astype(BF16)

    c = 0
    qa = proj(c, ATTN_WIDTH)
    for s in range(ATTN_WIDTH // LANES):
        cols = slice(s * LANES, (s + 1) * LANES)
        qa_ref[:, cols] = (_rope(qa[:, cols], cos, sin, first_half) * HEAD_DIM ** -0.5).astype(BF16)
    c += ATTN_WIDTH
    kv = proj(c, 2 * KV_WIDTH)
    for s in range(KV_WIDTH // LANES):
        dup_into(ka_ref, s, _rope(kv[:, s * LANES:(s + 1) * LANES], cos, sin, first_half))
        dup_into(va_ref, s, kv[:, KV_WIDTH + s * LANES:KV_WIDTH + (s + 1) * LANES])
    c += 2 * KV_WIDTH
    qg_ref[...] = proj(c, GLA_K_WIDTH)
    c += GLA_K_WIDTH
    kg_ref[...] = proj(c, GLA_K_WIDTH)
    c += GLA_K_WIDTH
    vg_ref[...] = proj(c, GLA_V_WIDTH).astype(BF16)
    c += GLA_V_WIDTH
    rg_ref[...] = proj(c, GLA_V_WIDTH).astype(BF16)
    c += GLA_V_WIDTH

    g_lr = proj(c, w_ref.shape[1] - c).astype(BF16)
    z = jnp.dot(g_lr, w2_ref[...], preferred_element_type=F32) + gb_ref[...]
    la_ref[...] = jax.nn.log_sigmoid(z) / GLA_TAU


def _inproj(x, gain, w_mix, w2, gate_b, cos, sin, layer):
    t, d = x.shape
    tm = PROJ_TM
    row = lambda width: pl.BlockSpec((tm, width), lambda i: (i, 0))
    widths = (ATTN_WIDTH, 2 * KV_WIDTH, 2 * KV_WIDTH, GLA_K_WIDTH, GLA_K_WIDTH, GLA_V_WIDTH,
              GLA_V_WIDTH, GLA_K_WIDTH)
    dtypes = (BF16, BF16, BF16, F32, F32, BF16, BF16, F32)
    return pl.pallas_call(
        _inproj_kernel,
        out_shape=[jax.ShapeDtypeStruct((t, w), dt) for w, dt in zip(widths, dtypes)],
        grid=(t // tm,),
        in_specs=[row(d), _resident((1, d)), _resident(w_mix.shape[1:], layer),
                  _resident(w2.shape[1:], layer), _resident((1, GLA_K_WIDTH)),
                  row(LANES), row(LANES)],
        out_specs=[row(w) for w in widths],
        compiler_params=_params("parallel"),
        name="mixer_inproj",
    )(x, gain, w_mix, w2, gate_b, cos, sin)


def _attn_kernel(sinks_ref, q_ref, kc_ref, vc_ref, kp_ref, vp_ref, o_ref):
    n = pl.program_id(1)
    blk = ATTN_BLOCK
    k_all = jnp.concatenate([kp_ref[...], kc_ref[...]], axis=0)
    v_all = jnp.concatenate([vp_ref[...], vc_ref[...]], axis=0)
    qi = lax.broadcasted_iota(jnp.int32, (2 * blk, blk), 0) % blk
    kj = lax.broadcasted_iota(jnp.int32, (2 * blk, blk), 1)
    top = lax.broadcasted_iota(jnp.int32, (2 * blk, 1), 0) < blk
    band_prev = kj > qi
    band_cur = kj <= qi
    band_prev_first = band_prev & (kj >= jnp.where(n > 0, 0, blk))
    lane = lax.broadcasted_iota(jnp.int32, (2 * blk, LANES), 1)
    key = lax.broadcasted_iota(jnp.int32, (2 * blk, LANES), 0)
    half = (lane < HEAD_DIM, lane >= HEAD_DIM)
    ones_half = tuple(jnp.where(hm, 1.0, 0.0).astype(BF16) for hm in half)
    zero = jnp.zeros((2 * blk, LANES), BF16)
    neg_inf = jnp.full((2 * blk, blk), -jnp.inf, F32)
    for g in range(N_KV_HEADS):
        gl = slice(g * LANES, (g + 1) * LANES)
        fills = []
        for parity in range(2):
            sink = jnp.where(top, sinks_ref[Q_PER_KV * g + parity], sinks_ref[Q_PER_KV * g + 2 + parity])
            fills.append(jnp.where(kj == 0, sink, neg_inf))
        for b in range(q_ref.shape[0] // blk):
            rows = slice(b * blk, (b + 1) * blk)
            q = jnp.concatenate([q_ref[rows, (2 * g) * LANES:(2 * g + 1) * LANES],
                                 q_ref[rows, (2 * g + 1) * LANES:(2 * g + 2) * LANES]], axis=0)
            kd = k_all[b * blk:(b + 2) * blk, gl]
            vd = v_all[b * blk:(b + 2) * blk, gl]
            mask_prev = band_prev_first if b == 0 else band_prev
            acc = None
            for parity in range(2):
                kz = jnp.where(half[parity], kd, zero)
                s = lax.dot_general(q, kz, (((1,), (1,)), ((), ())), preferred_element_type=F32)
                s_prev = jnp.where(mask_prev, s[:, :blk], fills[parity])
                s_cur = jnp.where(band_cur, s[:, blk:], neg_inf)
                m = jnp.max(jnp.maximum(s_prev, s_cur), axis=-1, keepdims=True)
                p = jnp.concatenate([jnp.exp(s_prev - m), jnp.exp(s_cur - m)], axis=1).astype(BF16)
                vz = jnp.where(half[parity] & (key > 0), vd, zero)
                w = jnp.concatenate([vz, ones_half[parity]], axis=1)
                part = jnp.dot(p, w, preferred_element_type=F32)
                acc = part if acc is None else acc + part
            out = (acc[:, :LANES] / acc[:, LANES:]).astype(BF16)
            o_ref[rows, (2 * g) * LANES:(2 * g + 1) * LANES] = out[:blk]
            o_ref[rows, (2 * g + 1) * LANES:(2 * g + 2) * LANES] = out[blk:]


def _attention(q, k, v, sinks, batch):
    t = q.shape[0]
    seq = t // batch
    tq = min(ATTN_TQ, seq)
    nq = seq // tq
    per = tq // ATTN_BLOCK
    kvw = k.shape[1]
    cur = lambda width: pl.BlockSpec((tq, width), lambda b, n: (b * nq + n, 0))
    prev = lambda width: pl.BlockSpec(
        (ATTN_BLOCK, width), lambda b, n: (jnp.maximum((b * nq + n) * per - 1, 0), 0))
    return pl.pallas_call(
        _attn_kernel,
        out_shape=jax.ShapeDtypeStruct((t, ATTN_WIDTH), BF16),
        grid=(batch, nq),
        in_specs=[pl.BlockSpec(memory_space=pltpu.SMEM),
                  cur(ATTN_WIDTH), cur(kvw), cur(kvw), prev(kvw), prev(kvw)],
        out_specs=cur(ATTN_WIDTH),
        compiler_params=_params("parallel", "arbitrary"),
        name="swa_attention",
    )(sinks, q, k, v, k, v)


def _gla_kernel(q_ref, k_ref, v_ref, la_ref, r_ref, gain_ref, o_ref, state_ref):
    @pl.when(pl.program_id(1) == 0)
    def _():
        state_ref[...] = jnp.zeros_like(state_ref)

    c = GLA_CHUNK
    ti = lax.broadcasted_iota(jnp.int32, (c, c), 0)
    si = lax.broadcasted_iota(jnp.int32, (c, c), 1)
    causal = ti >= si
    tri = causal.astype(F32)
    gain = gain_ref[...]
    scale = GLA_DK ** -0.5
    nt = (((1,), (1,)), ((), ()))
    state_t = [state_ref[h] for h in range(GLA_HEADS)]
    for ci in range(q_ref.shape[0] // c):
        rows = slice(ci * c, (ci + 1) * c)
        la = la_ref[rows, :]
        b = jnp.dot(tri, la, precision=lax.Precision.HIGHEST, preferred_element_type=F32)
        b_last = b[c - 1:c, :]
        q_dec = (q_ref[rows, :] * scale) * jnp.exp(b)
        k_all = k_ref[rows, :]
        k_dec = k_all * jnp.exp(-b)
        k_rem = k_all * jnp.exp(b_last - b)
        chunk_decay = jnp.exp(b_last)
        for h in range(GLA_HEADS):
            kl = slice(h * GLA_DK, (h + 1) * GLA_DK)
            vl = slice(h * GLA_DV, (h + 1) * GLA_DV)
            qh = q_dec[:, kl].astype(BF16)
            v_t = v_ref[rows, vl].astype(F32).T.astype(BF16)
            intra = lax.dot_general(qh, k_dec[:, kl].astype(BF16), nt, preferred_element_type=F32)
            intra = jnp.where(causal, intra, 0.0).astype(BF16)
            o = lax.dot_general(jnp.concatenate([qh, intra], axis=1),
                                jnp.concatenate([state_t[h].astype(BF16), v_t], axis=1),
                                nt, preferred_element_type=F32)
            d_state_t = jnp.dot(v_t, k_rem[:, kl].astype(BF16), preferred_element_type=F32)
            state_t[h] = state_t[h] * chunk_decay[:, kl] + d_state_t
            r = r_ref[rows, vl].astype(F32)
            o_ref[rows, vl] = (_rms(o, gain) * _silu(r)).astype(BF16)
    for h in range(GLA_HEADS):
        state_ref[h] = state_t[h]


def _gla(q, k, v, log_a, r, gain, batch):
    t = q.shape[0]
    seq = t // batch
    ts = min(GLA_TS, seq)
    ns = seq // ts
    row = lambda width: pl.BlockSpec((ts, width), lambda b, n: (b * ns + n, 0))
    return pl.pallas_call(
        _gla_kernel,
        out_shape=jax.ShapeDtypeStruct((t, GLA_V_WIDTH), BF16),
        grid=(batch, ns),
        in_specs=[row(GLA_K_WIDTH), row(GLA_K_WIDTH), row(GLA_V_WIDTH), row(GLA_K_WIDTH),
                  row(GLA_V_WIDTH), pl.BlockSpec((1, GLA_DV), lambda b, n: (0, 0))],
        out_specs=row(GLA_V_WIDTH),
        scratch_shapes=[pltpu.VMEM((GLA_HEADS, GLA_DV, GLA_DK), F32)],
        compiler_params=_params("parallel", "arbitrary"),
        name="gla",
    )(q, k, v, log_a, r, gain)


def _outproj_kernel(x_ref, a_ref, g_ref, w_ref, gain_ref, o_ref):
    part = o_ref.shape[0] // OUTPROJ_PARTS
    for r in range(OUTPROJ_PARTS):
        rows = slice(r * part, (r + 1) * part)
        h = (jnp.dot(a_ref[rows, :], w_ref[:ATTN_WIDTH, :], preferred_element_type=F32)
             + jnp.dot(g_ref[rows, :], w_ref[ATTN_WIDTH:, :], preferred_element_type=F32))
        o_ref[rows, :] = x_ref[rows, :] + _rms(h, gain_ref[...])


def _outproj(x, attn, gla, w_out, gain, layer):
    t, d = x.shape
    tm = PROJ_TM
    row = lambda width: pl.BlockSpec((tm, width), lambda i: (i, 0))
    return pl.pallas_call(
        _outproj_kernel,
        out_shape=jax.ShapeDtypeStruct((t, d), F32),
        grid=(t // tm,),
        in_specs=[row(d), row(ATTN_WIDTH), row(GLA_V_WIDTH), _resident(w_out.shape[1:], layer),
                  _resident((1, d))],
        out_specs=row(d),
        compiler_params=_params("parallel"),
        name="mixer_outproj",
    )(x, attn, gla, w_out, gain)


def _pad_to(a, axis, multiple):
    pad = (-a.shape[axis]) % multiple
    widths = [(0, 0)] * a.ndim
    widths[axis] = (0, pad)
    return jnp.pad(a, widths)


def _prep_ffn_weights(ffn_w_in, ffn_w_out):
    full = (ffn_w_out.shape[-2] // FFN_TF) * FFN_TF
    wot = _pad_to(ffn_w_out[..., full:, :], -2, FFN_TF).astype(BF16)
    return _cast_w_in(ffn_w_in), ffn_w_out.astype(BF16), wot


def _prep_mixer_weights(w_mix_in, gla_gate_w2, w_mix_out):
    w_mix = _pad_to(w_mix_in, -1, 2 * LANES).astype(BF16)
    main = w_mix_in.shape[-1] - GLA_GATE_RANK
    w2 = jnp.pad(gla_gate_w2, ((0, 0), (0, w_mix.shape[-1] - main - GLA_GATE_RANK), (0, 0))).astype(BF16)
    return w_mix, w2, w_mix_out.astype(BF16)


def kernel(x, positions, norm_gains, ffn_w_in, ffn_w_out, w_mix_in, attn_sinks, gla_gate_w2,
           gla_gate_b, gla_norm_gain, w_mix_out):
    batch, seq, d = x.shape
    depth = norm_gains.shape[0]
    xt = x.reshape(batch * seq, d)
    cos, sin = _rope_tables(positions)
    w_ffn = _prep_ffn_weights(ffn_w_in, ffn_w_out)
    w_mix, w2, w_mo = _prep_mixer_weights(w_mix_in, gla_gate_w2, w_mix_out)
    for l in range(depth):
        g = norm_gains[l].reshape(-1, 1, d)
        xt = _ffn(xt, g[0], g[1], w_ffn, l, 0)
        qa, ka, va, qg, kg, vg, rg, la = _inproj(
            xt, g[2], w_mix, w2, gla_gate_b[l].reshape(1, -1), cos, sin, l)
        attn = _attention(qa, ka, va, attn_sinks[l], batch)
        gla = _gla(qg, kg, vg, la, rg, gla_norm_gain[l].reshape(1, -1), batch)
        xt = _outproj(xt, attn, gla, w_mo, g[3], l)
        xt = _ffn(xt, g[4], g[5], w_ffn, l, 1)
    return xt.reshape(batch, seq, d)
```

```python
import functools

import numpy as np
import jax
import jax.numpy as jnp
from jax import lax
from jax.experimental import pallas as pl
from jax.experimental.pallas import tpu as pltpu

F32 = jnp.float32
BF16 = jnp.bfloat16

HEAD_DIM = 64
N_Q_HEADS = 16
N_KV_HEADS = 4
Q_PER_KV = N_Q_HEADS // N_KV_HEADS
ATTN_WIDTH = N_Q_HEADS * HEAD_DIM
KV_WIDTH = N_KV_HEADS * HEAD_DIM
ATTN_BLOCK = 128
ROPE_THETA = 10000.0
GLA_HEADS = 4
GLA_DK = 128
GLA_DV = 256
GLA_K_WIDTH = GLA_HEADS * GLA_DK
GLA_V_WIDTH = GLA_HEADS * GLA_DV
GLA_GATE_RANK = 16
GLA_TAU = 16.0
GLA_CHUNK = 64
NORM_EPS = 1e-6

LANES = 128
VMEM_LIMIT_BYTES = 60000 * 1024

FFN_TM = 1024
FFN_TF = 512
FFN_LAST_PARTS = 4
PROJ_TM = 512
OUTPROJ_PARTS = 1
ATTN_TQ = 512
GLA_TS = 512
CAST_ROWS = 256


def _params(*sem):
    return pltpu.CompilerParams(dimension_semantics=sem, vmem_limit_bytes=VMEM_LIMIT_BYTES)


def _resident(shape, lead=()):
    lead = tuple(lead) if isinstance(lead, (tuple, list)) else (lead,)
    return pl.BlockSpec((None,) * len(lead) + tuple(shape), lambda *_: lead + (0,) * len(shape),
                        pipeline_mode=pl.Buffered(1))


def _rms(x, gain):
    ms = jnp.mean(x * x, axis=-1, keepdims=True)
    return (x * lax.rsqrt(ms + NORM_EPS)) * gain


def _silu(x):
    return x * jax.nn.sigmoid(x)


def _swiglu_chunk(hn, wgu, wo):
    gu = jnp.dot(hn, wgu, preferred_element_type=F32)
    act = (_silu(gu[:, :FFN_TF]) * gu[:, FFN_TF:]).astype(BF16)
    return jnp.dot(act, wo, preferred_element_type=F32)


def _ffn_kernel(x_ref, gin_ref, gout_ref, wgu_ref, wo_ref, wot_ref, o_ref, hn_ref):
    j = pl.program_id(1)
    last = pl.num_programs(1) - 1

    @pl.when(j == 0)
    def _():
        hn = _rms(x_ref[...], gin_ref[...]).astype(BF16)
        hn_ref[...] = hn
        o_ref[...] = _swiglu_chunk(hn, wgu_ref[...], wo_ref[...])

    @pl.when((j > 0) & (j < last))
    def _():
        o_ref[...] += _swiglu_chunk(hn_ref[...], wgu_ref[...], wo_ref[...])

    @pl.when(j == last)
    def _():
        part = o_ref.shape[0] // FFN_LAST_PARTS
        for r in range(FFN_LAST_PARTS):
            rows = slice(r * part, (r + 1) * part)
            h = o_ref[rows, :] + _swiglu_chunk(hn_ref[rows, :], wgu_ref[...], wot_ref[...])
            o_ref[rows, :] = x_ref[rows, :] + 0.5 * _rms(h, gout_ref[...])


def _ffn(x, gin, gout, w, layer, slot):
    wgu, wo, wot = w
    t, d = x.shape
    nc = wgu.shape[2]
    n_full = wo.shape[2] // FFN_TF
    assert nc >= 2 and n_full == nc - 1
    return pl.pallas_call(
        _ffn_kernel,
        out_shape=jax.ShapeDtypeStruct((t, d), F32),
        grid=(t // FFN_TM, nc),
        in_specs=[
            pl.BlockSpec((FFN_TM, d), lambda i, j: (i, 0)),
            pl.BlockSpec((1, d), lambda i, j: (0, 0)),
            pl.BlockSpec((1, d), lambda i, j: (0, 0)),
            pl.BlockSpec((None, None, None, d, 2 * FFN_TF), lambda i, j: (layer, slot, j, 0, 0)),
            pl.BlockSpec((None, None, FFN_TF, d),
                         lambda i, j: (layer, slot, jnp.minimum(j, n_full - 1), 0)),
            _resident((FFN_TF, d), (layer, slot)),
        ],
        out_specs=pl.BlockSpec((FFN_TM, d), lambda i, j: (i, 0)),
        scratch_shapes=[pltpu.VMEM((FFN_TM, d), BF16)],
        compiler_params=_params("parallel", "arbitrary"),
        name="ffn",
    )(x, gin, gout, wgu, wo, wot)


def _cast_w_in_kernel(w_ref, dst):
    d_ff = w_ref.shape[1] // 2
    for c in range(dst.shape[0]):
        lo = c * FFN_TF
        width = min(FFN_TF, d_ff - lo)
        for half in range(2):
            src = half * d_ff + lo
            dst[c, :, half * FFN_TF:half * FFN_TF + width] = w_ref[:, src:src + width].astype(BF16)
            if width < FFN_TF:
                dst[c, :, half * FFN_TF + width:(half + 1) * FFN_TF] = jnp.zeros(
                    (dst.shape[1], FFN_TF - width), BF16)


def _cast_w_in(ffn_w_in):
    depth, two, d, f2 = ffn_w_in.shape
    nc = pl.cdiv(f2 // 2, FFN_TF)
    rt = CAST_ROWS
    return pl.pallas_call(
        _cast_w_in_kernel,
        out_shape=jax.ShapeDtypeStruct((depth, two, nc, d, 2 * FFN_TF), BF16),
        grid=(depth * two, d // rt),
        in_specs=[pl.BlockSpec((None, None, rt, f2), lambda m, i: (m // two, m % two, i, 0))],
        out_specs=pl.BlockSpec((None, None, nc, rt, 2 * FFN_TF),
                               lambda m, i: (m // two, m % two, 0, i, 0)),
        compiler_params=_params("parallel", "parallel"),
        name="cast_ffn_w_in",
    )(ffn_w_in)


def _rope_table_kernel(pos_ref, invf_ref, cos_ref, sin_ref):
    ang = pos_ref[...].astype(F32) * invf_ref[...]
    lane = lax.broadcasted_iota(jnp.int32, ang.shape, 1)
    first_half = (lane % HEAD_DIM) < (HEAD_DIM // 2)
    cos_ref[...] = jnp.cos(ang)
    sin_ref[...] = jnp.where(first_half, -jnp.sin(ang), jnp.sin(ang))


def _rope_tables(positions):
    t = positions.size
    tm = min(t, 2048)
    inv_freq = ROPE_THETA ** (-jnp.arange(0, HEAD_DIM, 2, dtype=F32) / HEAD_DIM)
    invf = jnp.tile(inv_freq, LANES // (HEAD_DIM // 2)).reshape(1, LANES)
    return pl.pallas_call(
        _rope_table_kernel,
        out_shape=[jax.ShapeDtypeStruct((t, LANES), F32)] * 2,
        grid=(t // tm,),
        in_specs=[pl.BlockSpec((tm, 1), lambda i: (i, 0)),
                  pl.BlockSpec((1, LANES), lambda i: (0, 0))],
        out_specs=[pl.BlockSpec((tm, LANES), lambda i: (i, 0))] * 2,
        compiler_params=_params("parallel"),
        name="rope_tables",
    )(positions.reshape(t, 1), invf)


def _rope(x, cos, sin, first_half):
    half = HEAD_DIM // 2
    partner = jnp.where(first_half, pltpu.roll(x, LANES - half, axis=1), pltpu.roll(x, half, axis=1))
    return x * cos + partner * sin


def _inproj_kernel(x_ref, g_ref, w_ref, w2_ref, gb_ref, cos_ref, sin_ref,
                   qa_ref, ka_ref, va_ref, qg_ref, kg_ref, vg_ref, rg_ref, la_ref):
    hn = _rms(x_ref[...], g_ref[...]).astype(BF16)
    cos = cos_ref[...]
    sin = sin_ref[...]
    lane = lax.broadcasted_iota(jnp.int32, cos.shape, 1)
    first_half = (lane % HEAD_DIM) < (HEAD_DIM // 2)

    def proj(c0, width):
        return jnp.dot(hn, w_ref[:, c0:c0 + width], preferred_element_type=F32)

    def dup_into(dst_ref, s, slab):
        lo = lane < HEAD_DIM
        swapped = pltpu.roll(slab, HEAD_DIM, axis=1)
        dst_ref[:, (2 * s) * LANES:(2 * s + 1) * LANES] = jnp.where(lo, slab, swapped).astype(BF16)
        dst_ref[:, (2 * s + 1) * LANES:(2 * s + 2) * LANES] = jnp.where(lo, swapped, slab).astype(BF16)

    c = 0
    qa = proj(c, ATTN_WIDTH)
    for s in range(ATTN_WIDTH // LANES):
        cols = slice(s * LANES, (s + 1) * LANES)
        qa_ref[:, cols] = (_rope(qa[:, cols], cos, sin, first_half) * HEAD_DIM ** -0.5).astype(BF16)
    c += ATTN_WIDTH
    kv = proj(c, 2 * KV_WIDTH)
    for s in range(KV_WIDTH // LANES):
        dup_into(ka_ref, s, _rope(kv[:, s * LANES:(s + 1) * LANES], cos, sin, first_half))
        dup_into(va_ref, s, kv[:, KV_WIDTH + s * LANES:KV_WIDTH + (s + 1) * LANES])
    c += 2 * KV_WIDTH
    qg_ref[...] = proj(c, GLA_K_WIDTH)
    c += GLA_K_WIDTH
    kg_ref[...] = proj(c, GLA_K_WIDTH)
    c += GLA_K_WIDTH
    vg_ref[...] = proj(c, GLA_V_WIDTH).astype(BF16)
    c += GLA_V_WIDTH
    rg_ref[...] = proj(c, GLA_V_WIDTH).astype(BF16)
    c += GLA_V_WIDTH

    g_lr = proj(c, w_ref.shape[1] - c).astype(BF16)
    z = jnp.dot(g_lr, w2_ref[...], preferred_element_type=F32) + gb_ref[...]
    la_ref[...] = jax.nn.log_sigmoid(z) / GLA_TAU


def _inproj(x, gain, w_mix, w2, gate_b, cos, sin, layer):
    t, d = x.shape
    tm = PROJ_TM
    row = lambda width: pl.BlockSpec((tm, width), lambda i: (i, 0))
    widths = (ATTN_WIDTH, 2 * KV_WIDTH, 2 * KV_WIDTH, GLA_K_WIDTH, GLA_K_WIDTH, GLA_V_WIDTH,
              GLA_V_WIDTH, GLA_K_WIDTH)
    dtypes = (BF16, BF16, BF16, F32, F32, BF16, BF16, F32)
    return pl.pallas_call(
        _inproj_kernel,
        out_shape=[jax.ShapeDtypeStruct((t, w), dt) for w, dt in zip(widths, dtypes)],
        grid=(t // tm,),
        in_specs=[row(d), _resident((1, d)), _resident(w_mix.shape[1:], layer),
                  _resident(w2.shape[1:], layer), _resident((1, GLA_K_WIDTH)),
                  row(LANES), row(LANES)],
        out_specs=[row(w) for w in widths],
        compiler_params=_params("parallel"),
        name="mixer_inproj",
    )(x, gain, w_mix, w2, gate_b, cos, sin)


def _attn_kernel(sinks_ref, q_ref, kc_ref, vc_ref, kp_ref, vp_ref, o_ref):
    n = pl.program_id(1)
    blk = ATTN_BLOCK
    k_all = jnp.concatenate([kp_ref[...], kc_ref[...]], axis=0)
    v_all = jnp.concatenate([vp_ref[...], vc_ref[...]], axis=0)
    qi = lax.broadcasted_iota(jnp.int32, (2 * blk, blk), 0) % blk
    kj = lax.broadcasted_iota(jnp.int32, (2 * blk, blk), 1)
    top = lax.broadcasted_iota(jnp.int32, (2 * blk, 1), 0) < blk
    band_prev = kj > qi
    band_cur = kj <= qi
    band_prev_first = band_prev & (kj >= jnp.where(n > 0, 0, blk))
    lane = lax.broadcasted_iota(jnp.int32, (2 * blk, LANES), 1)
    key = lax.broadcasted_iota(jnp.int32, (2 * blk, LANES), 0)
    half = (lane < HEAD_DIM, lane >= HEAD_DIM)
    ones_half = tuple(jnp.where(hm, 1.0, 0.0).astype(BF16) for hm in half)
    zero = jnp.zeros((2 * blk, LANES), BF16)
    neg_inf = jnp.full((2 * blk, blk), -jnp.inf, F32)
    for g in range(N_KV_HEADS):
        gl = slice(g * LANES, (g + 1) * LANES)
        fills = []
        for parity in range(2):
            sink = jnp.where(top, sinks_ref[Q_PER_KV * g + parity], sinks_ref[Q_PER_KV * g + 2 + parity])
            fills.append(jnp.where(kj == 0, sink, neg_inf))
        for b in range(q_ref.shape[0] // blk):
            rows = slice(b * blk, (b + 1) * blk)
            q = jnp.concatenate([q_ref[rows, (2 * g) * LANES:(2 * g + 1) * LANES],
                                 q_ref[rows, (2 * g + 1) * LANES:(2 * g + 2) * LANES]], axis=0)
            kd = k_all[b * blk:(b + 2) * blk, gl]
            vd = v_all[b * blk:(b + 2) * blk, gl]
            mask_prev = band_prev_first if b == 0 else band_prev
            acc = None
            for parity in range(2):
                kz = jnp.where(half[parity], kd, zero)
                s = lax.dot_general(q, kz, (((1,), (1,)), ((), ())), preferred_element_type=F32)
                s_prev = jnp.where(mask_prev, s[:, :blk], fills[parity])
                s_cur = jnp.where(band_cur, s[:, blk:], neg_inf)
                m = jnp.max(jnp.maximum(s_prev, s_cur), axis=-1, keepdims=True)
                p = jnp.concatenate([jnp.exp(s_prev - m), jnp.exp(s_cur - m)], axis=1).astype(BF16)
                vz = jnp.where(half[parity] & (key > 0), vd, zero)
                w = jnp.concatenate([vz, ones_half[parity]], axis=1)
                part = jnp.dot(p, w, preferred_element_type=F32)
                acc = part if acc is None else acc + part
            out = (acc[:, :LANES] / acc[:, LANES:]).astype(BF16)
            o_ref[rows, (2 * g) * LANES:(2 * g + 1) * LANES] = out[:blk]
            o_ref[rows, (2 * g + 1) * LANES:(2 * g + 2) * LANES] = out[blk:]


def _attention(q, k, v, sinks, batch):
    t = q.shape[0]
    seq = t // batch
    tq = min(ATTN_TQ, seq)
    nq = seq // tq
    per = tq // ATTN_BLOCK
    kvw = k.shape[1]
    cur = lambda width: pl.BlockSpec((tq, width), lambda b, n: (b * nq + n, 0))
    prev = lambda width: pl.BlockSpec(
        (ATTN_BLOCK, width), lambda b, n: (jnp.maximum((b * nq + n) * per - 1, 0), 0))
    return pl.pallas_call(
        _attn_kernel,
        out_shape=jax.ShapeDtypeStruct((t, ATTN_WIDTH), BF16),
        grid=(batch, nq),
        in_specs=[pl.BlockSpec(memory_space=pltpu.SMEM),
                  cur(ATTN_WIDTH), cur(kvw), cur(kvw), prev(kvw), prev(kvw)],
        out_specs=cur(ATTN_WIDTH),
        compiler_params=_params("parallel", "arbitrary"),
        name="swa_attention",
    )(sinks, q, k, v, k, v)


def _gla_kernel(q_ref, k_ref, v_ref, la_ref, r_ref, gain_ref, o_ref, state_ref):
    @pl.when(pl.program_id(1) == 0)
    def _():
        state_ref[...] = jnp.zeros_like(state_ref)

    c = GLA_CHUNK
    pc = 2 * c
    ti = lax.broadcasted_iota(jnp.int32, (pc, pc), 0)
    si = lax.broadcasted_iota(jnp.int32, (pc, pc), 1)
    same_chunk = (ti >= c) == (si >= c)
    causal = same_chunk & (ti >= si)
    cross = (ti >= c) & (si < c)
    tri = causal.astype(F32)
    second = lax.broadcasted_iota(jnp.int32, (pc, 1), 0) >= c
    gain = gain_ref[...]
    scale = GLA_DK ** -0.5
    nt = (((1,), (1,)), ((), ()))
    state_t = [state_ref[h] for h in range(GLA_HEADS)]
    for pi in range(q_ref.shape[0] // pc):
        rows = slice(pi * pc, (pi + 1) * pc)
        la = la_ref[rows, :]
        b = jnp.dot(tri, la, precision=lax.Precision.HIGHEST, preferred_element_type=F32)
        b_last0 = b[c - 1:c, :]
        b_last1 = b[pc - 1:pc, :]
        decay0 = jnp.exp(b_last0)
        decay1 = jnp.exp(b_last1)
        q_dec = (q_ref[rows, :] * scale) * jnp.exp(b)
        k_all = k_ref[rows, :]
        k_dec = k_all * jnp.exp(-b)
        k_rem = k_all * jnp.exp(jnp.where(second, b_last1, b_last0) - b)
        q_in = jnp.where(second, q_dec * decay0, q_dec)
        k_out = jnp.where(second, k_rem, k_rem * decay1)
        for h in range(GLA_HEADS):
            kl = slice(h * GLA_DK, (h + 1) * GLA_DK)
            vl = slice(h * GLA_DV, (h + 1) * GLA_DV)
            qh = q_dec[:, kl].astype(BF16)
            qih = q_in[:, kl].astype(BF16)
            v_t = v_ref[rows, vl].astype(F32).T.astype(BF16)
            s2 = lax.dot_general(jnp.concatenate([qh, qih], axis=0), k_dec[:, kl].astype(BF16), nt,
                                 preferred_element_type=F32)
            intra = (jnp.where(causal, s2[:pc], 0.0) + jnp.where(cross, s2[pc:], 0.0)).astype(BF16)
            o = lax.dot_general(jnp.concatenate([qih, intra], axis=1),
                                jnp.concatenate([state_t[h].astype(BF16), v_t], axis=1),
                                nt, preferred_element_type=F32)
            d_state_t = jnp.dot(v_t, k_out[:, kl].astype(BF16), preferred_element_type=F32)
            state_t[h] = state_t[h] * (decay0[:, kl] * decay1[:, kl]) + d_state_t
            r = r_ref[rows, vl].astype(F32)
            o_ref[rows, vl] = (_rms(o, gain) * _silu(r)).astype(BF16)
    for h in range(GLA_HEADS):
        state_ref[h] = state_t[h]


def _gla(q, k, v, log_a, r, gain, batch):
    t = q.shape[0]
    seq = t // batch
    ts = min(GLA_TS, seq)
    ns = seq // ts
    row = lambda width: pl.BlockSpec((ts, width), lambda b, n: (b * ns + n, 0))
    return pl.pallas_call(
        _gla_kernel,
        out_shape=jax.ShapeDtypeStruct((t, GLA_V_WIDTH), BF16),
        grid=(batch, ns),
        in_specs=[row(GLA_K_WIDTH), row(GLA_K_WIDTH), row(GLA_V_WIDTH), row(GLA_K_WIDTH),
                  row(GLA_V_WIDTH), pl.BlockSpec((1, GLA_DV), lambda b, n: (0, 0))],
        out_specs=row(GLA_V_WIDTH),
        scratch_shapes=[pltpu.VMEM((GLA_HEADS, GLA_DV, GLA_DK), F32)],
        compiler_params=_params("parallel", "arbitrary"),
        name="gla",
    )(q, k, v, log_a, r, gain)


def _outproj_kernel(x_ref, a_ref, g_ref, w_ref, gain_ref, o_ref):
    part = o_ref.shape[0] // OUTPROJ_PARTS
    for r in range(OUTPROJ_PARTS):
        rows = slice(r * part, (r + 1) * part)
        h = (jnp.dot(a_ref[rows, :], w_ref[:ATTN_WIDTH, :], preferred_element_type=F32)
             + jnp.dot(g_ref[rows, :], w_ref[ATTN_WIDTH:, :], preferred_element_type=F32))
        o_ref[rows, :] = x_ref[rows, :] + _rms(h, gain_ref[...])


def _outproj(x, attn, gla, w_out, gain, layer):
    t, d = x.shape
    tm = PROJ_TM
    row = lambda width: pl.BlockSpec((tm, width), lambda i: (i, 0))
    return pl.pallas_call(
        _outproj_kernel,
        out_shape=jax.ShapeDtypeStruct((t, d), F32),
        grid=(t // tm,),
        in_specs=[row(d), row(ATTN_WIDTH), row(GLA_V_WIDTH), _resident(w_out.shape[1:], layer),
                  _resident((1, d))],
        out_specs=row(d),
        compiler_params=_params("parallel"),
        name="mixer_outproj",
    )(x, attn, gla, w_out, gain)


def _pad_to(a, axis, multiple):
    pad = (-a.shape[axis]) % multiple
    widths = [(0, 0)] * a.ndim
    widths[axis] = (0, pad)
    return jnp.pad(a, widths)


def _prep_ffn_weights(ffn_w_in, ffn_w_out):
    full = (ffn_w_out.shape[-2] // FFN_TF) * FFN_TF
    wot = _pad_to(ffn_w_out[..., full:, :], -2, FFN_TF).astype(BF16)
    return _cast_w_in(ffn_w_in), ffn_w_out.astype(BF16), wot


def _prep_mixer_weights(w_mix_in, gla_gate_w2, w_mix_out):
    w_mix = _pad_to(w_mix_in, -1, 2 * LANES).astype(BF16)
    main = w_mix_in.shape[-1] - GLA_GATE_RANK
    w2 = jnp.pad(gla_gate_w2, ((0, 0), (0, w_mix.shape[-1] - main - GLA_GATE_RANK), (0, 0))).astype(BF16)
    return w_mix, w2, w_mix_out.astype(BF16)


def kernel(x, positions, norm_gains, ffn_w_in, ffn_w_out, w_mix_in, attn_sinks, gla_gate_w2,
           gla_gate_b, gla_norm_gain, w_mix_out):
    batch, seq, d = x.shape
    depth = norm_gains.shape[0]
    xt = x.reshape(batch * seq, d)
    cos, sin = _rope_tables(positions)
    w_ffn = _prep_ffn_weights(ffn_w_in, ffn_w_out)
    w_mix, w2, w_mo = _prep_mixer_weights(w_mix_in, gla_gate_w2, w_mix_out)
    for l in range(depth):
        g = norm_gains[l].reshape(-1, 1, d)
        xt = _ffn(xt, g[0], g[1], w_ffn, l, 0)
        qa, ka, va, qg, kg, vg, rg, la = _inproj(
            xt, g[2], w_mix, w2, gla_gate_b[l].reshape(1, -1), cos, sin, l)
        attn = _attention(qa, ka, va, attn_sinks[l], batch)
        gla = _gla(qg, kg, vg, la, rg, gla_norm_gain[l].reshape(1, -1), batch)
        xt = _outproj(xt, attn, gla, w_mo, g[3], l)
        xt = _ffn(xt, g[4], g[5], w_ffn, l, 1)
    return xt.reshape(batch, seq, d)
```

```python
import functools

import numpy as np
import jax
import jax.numpy as jnp
from jax import lax
from jax.experimental import pallas as pl
from jax.experimental.pallas import tpu as pltpu

F32 = jnp.float32
BF16 = jnp.bfloat16

HEAD_DIM = 64
N_Q_HEADS = 16
N_KV_HEADS = 4
Q_PER_KV = N_Q_HEADS // N_KV_HEADS
ATTN_WIDTH = N_Q_HEADS * HEAD_DIM
KV_WIDTH = N_KV_HEADS * HEAD_DIM
ATTN_BLOCK = 128
ROPE_THETA = 10000.0
GLA_HEADS = 4
GLA_DK = 128
GLA_DV = 256
GLA_K_WIDTH = GLA_HEADS * GLA_DK
GLA_V_WIDTH = GLA_HEADS * GLA_DV
GLA_GATE_RANK = 16
GLA_TAU = 16.0
GLA_CHUNK = 64
NORM_EPS = 1e-6

LANES = 128
VMEM_LIMIT_BYTES = 60000 * 1024

FFN_TM = 1024
FFN_TF = 512
FFN_LAST_PARTS = 4
PROJ_TM = 512
OUTPROJ_PARTS = 1
ATTN_TQ = 512
GLA_TS = 512
CAST_ROWS = 256
CAST_OUT_STEPS = 8


def _params(*sem):
    return pltpu.CompilerParams(dimension_semantics=sem, vmem_limit_bytes=VMEM_LIMIT_BYTES)


def _resident(shape, lead=()):
    lead = tuple(lead) if isinstance(lead, (tuple, list)) else (lead,)
    return pl.BlockSpec((None,) * len(lead) + tuple(shape), lambda *_: lead + (0,) * len(shape),
                        pipeline_mode=pl.Buffered(1))


def _rms(x, gain):
    ms = jnp.mean(x * x, axis=-1, keepdims=True)
    return (x * lax.rsqrt(ms + NORM_EPS)) * gain


def _silu(x):
    return x * jax.nn.sigmoid(x)


def _swiglu_chunk(hn, wgu, wo):
    width = wo.shape[0]
    gu = jnp.dot(hn, wgu, preferred_element_type=F32)
    act = (_silu(gu[:, :width]) * gu[:, width:]).astype(BF16)
    return jnp.dot(act, wo, preferred_element_type=F32)


def _ffn_kernel(x_ref, gin_ref, gout_ref, wgu_ref, wo_ref, wgut_ref, wot_ref, o_ref, hn_ref):
    j = pl.program_id(1)
    last = pl.num_programs(1) - 1

    @pl.when(j == 0)
    def _():
        hn = _rms(x_ref[...], gin_ref[...]).astype(BF16)
        hn_ref[...] = hn
        o_ref[...] = _swiglu_chunk(hn, wgu_ref[...], wo_ref[...])

    @pl.when((j > 0) & (j < last))
    def _():
        o_ref[...] += _swiglu_chunk(hn_ref[...], wgu_ref[...], wo_ref[...])

    @pl.when(j == last)
    def _():
        part = o_ref.shape[0] // FFN_LAST_PARTS
        for r in range(FFN_LAST_PARTS):
            rows = slice(r * part, (r + 1) * part)
            h = o_ref[rows, :] + _swiglu_chunk(hn_ref[rows, :], wgut_ref[...], wot_ref[...])
            o_ref[rows, :] = x_ref[rows, :] + 0.5 * _rms(h, gout_ref[...])


def _ffn(x, gin, gout, w, layer, slot):
    wgu, wo, wgut, wot = w
    t, d = x.shape
    n_full = wgu.shape[2]
    tail = wot.shape[2]
    assert n_full >= 2 and wo.shape[2] == n_full * FFN_TF + tail and 0 < tail and tail % LANES == 0
    main = lambda j: jnp.minimum(j, n_full - 1)
    return pl.pallas_call(
        _ffn_kernel,
        out_shape=jax.ShapeDtypeStruct((t, d), F32),
        grid=(t // FFN_TM, n_full + 1),
        in_specs=[
            pl.BlockSpec((FFN_TM, d), lambda i, j: (i, 0)),
            pl.BlockSpec((1, d), lambda i, j: (0, 0)),
            pl.BlockSpec((1, d), lambda i, j: (0, 0)),
            pl.BlockSpec((None, None, None, d, 2 * FFN_TF),
                         lambda i, j: (layer, slot, main(j), 0, 0)),
            pl.BlockSpec((None, None, FFN_TF, d), lambda i, j: (layer, slot, main(j), 0)),
            _resident((d, 2 * tail), (layer, slot)),
            _resident((tail, d), (layer, slot)),
        ],
        out_specs=pl.BlockSpec((FFN_TM, d), lambda i, j: (i, 0)),
        scratch_shapes=[pltpu.VMEM((FFN_TM, d), BF16)],
        compiler_params=_params("parallel", "arbitrary"),
        name="ffn",
    )(x, gin, gout, wgu, wo, wgut, wot)


def _cast_w_in_kernel(w_ref, main_ref, tail_ref):
    d_ff = w_ref.shape[1] // 2
    for c in range(main_ref.shape[0]):
        for half in range(2):
            src = half * d_ff + c * FFN_TF
            main_ref[c, :, half * FFN_TF:(half + 1) * FFN_TF] = w_ref[:, src:src + FFN_TF].astype(BF16)
    tail = tail_ref.shape[1] // 2
    for half in range(2):
        src = (half + 1) * d_ff - tail
        tail_ref[:, half * tail:(half + 1) * tail] = w_ref[:, src:src + tail].astype(BF16)


def _cast_w_in(ffn_w_in):
    depth, two, d, f2 = ffn_w_in.shape
    n_full = (f2 // 2) // FFN_TF
    tail = f2 // 2 - n_full * FFN_TF
    rt = CAST_ROWS
    pick = lambda m, i: (m // two, m % two)
    return pl.pallas_call(
        _cast_w_in_kernel,
        out_shape=[jax.ShapeDtypeStruct((depth, two, n_full, d, 2 * FFN_TF), BF16),
                   jax.ShapeDtypeStruct((depth, two, d, 2 * tail), BF16)],
        grid=(depth * two, d // rt),
        in_specs=[pl.BlockSpec((None, None, rt, f2), lambda m, i: pick(m, i) + (i, 0))],
        out_specs=[pl.BlockSpec((None, None, n_full, rt, 2 * FFN_TF), lambda m, i: pick(m, i) + (0, i, 0)),
                   pl.BlockSpec((None, None, rt, 2 * tail), lambda m, i: pick(m, i) + (i, 0))],
        compiler_params=_params("parallel", "parallel"),
        name="cast_ffn_w_in",
    )(ffn_w_in)


def _cast_rows_kernel(w_ref, o_ref):
    o_ref[...] = w_ref[...].astype(BF16)


def _cast_w_out(ffn_w_out):
    depth, two, f, d = ffn_w_out.shape
    rt = f // CAST_OUT_STEPS
    assert rt * CAST_OUT_STEPS == f and rt % 16 == 0
    spec = pl.BlockSpec((None, None, rt, d), lambda m, i: (m // two, m % two, i, 0))
    return pl.pallas_call(
        _cast_rows_kernel,
        out_shape=jax.ShapeDtypeStruct(ffn_w_out.shape, BF16),
        grid=(depth * two, CAST_OUT_STEPS),
        in_specs=[spec],
        out_specs=spec,
        compiler_params=_params("parallel", "parallel"),
        name="cast_ffn_w_out",
    )(ffn_w_out)


def _rope_table_kernel(pos_ref, invf_ref, cos_ref, sin_ref):
    ang = pos_ref[...].astype(F32) * invf_ref[...]
    lane = lax.broadcasted_iota(jnp.int32, ang.shape, 1)
    first_half = (lane % HEAD_DIM) < (HEAD_DIM // 2)
    cos_ref[...] = jnp.cos(ang)
    sin_ref[...] = jnp.where(first_half, -jnp.sin(ang), jnp.sin(ang))


def _rope_tables(positions):
    t = positions.size
    tm = min(t, 2048)
    inv_freq = ROPE_THETA ** (-jnp.arange(0, HEAD_DIM, 2, dtype=F32) / HEAD_DIM)
    invf = jnp.tile(inv_freq, LANES // (HEAD_DIM // 2)).reshape(1, LANES)
    return pl.pallas_call(
        _rope_table_kernel,
        out_shape=[jax.ShapeDtypeStruct((t, LANES), F32)] * 2,
        grid=(t // tm,),
        in_specs=[pl.BlockSpec((tm, 1), lambda i: (i, 0)),
                  pl.BlockSpec((1, LANES), lambda i: (0, 0))],
        out_specs=[pl.BlockSpec((tm, LANES), lambda i: (i, 0))] * 2,
        compiler_params=_params("parallel"),
        name="rope_tables",
    )(positions.reshape(t, 1), invf)


def _rope(x, cos, sin, first_half):
    half = HEAD_DIM // 2
    partner = jnp.where(first_half, pltpu.roll(x, LANES - half, axis=1), pltpu.roll(x, half, axis=1))
    return x * cos + partner * sin


def _inproj_kernel(x_ref, g_ref, w_ref, w2_ref, gb_ref, cos_ref, sin_ref,
                   qa_ref, ka_ref, va_ref, qg_ref, kg_ref, vg_ref, rg_ref, la_ref):
    hn = _rms(x_ref[...], g_ref[...]).astype(BF16)
    cos = cos_ref[...]
    sin = sin_ref[...]
    lane = lax.broadcasted_iota(jnp.int32, cos.shape, 1)
    first_half = (lane % HEAD_DIM) < (HEAD_DIM // 2)

    def proj(c0, width):
        return jnp.dot(hn, w_ref[:, c0:c0 + width], preferred_element_type=F32)

    def dup_into(dst_ref, s, slab):
        lo = lane < HEAD_DIM
        swapped = pltpu.roll(slab, HEAD_DIM, axis=1)
        dst_ref[:, (2 * s) * LANES:(2 * s + 1) * LANES] = jnp.where(lo, slab, swapped).astype(BF16)
        dst_ref[:, (2 * s + 1) * LANES:(2 * s + 2) * LANES] = jnp.where(lo, swapped, slab).astype(BF16)

    c = 0
    qa = proj(c, ATTN_WIDTH)
    for s in range(ATTN_WIDTH // LANES):
        cols = slice(s * LANES, (s + 1) * LANES)
        qa_ref[:, cols] = (_rope(qa[:, cols], cos, sin, first_half) * HEAD_DIM ** -0.5).astype(BF16)
    c += ATTN_WIDTH
    kv = proj(c, 2 * KV_WIDTH)
    for s in range(KV_WIDTH // LANES):
        dup_into(ka_ref, s, _rope(kv[:, s * LANES:(s + 1) * LANES], cos, sin, first_half))
        dup_into(va_ref, s, kv[:, KV_WIDTH + s * LANES:KV_WIDTH + (s + 1) * LANES])
    c += 2 * KV_WIDTH
    qg_ref[...] = proj(c, GLA_K_WIDTH)
    c += GLA_K_WIDTH
    kg_ref[...] = proj(c, GLA_K_WIDTH)
    c += GLA_K_WIDTH
    vg_ref[...] = proj(c, GLA_V_WIDTH).astype(BF16)
    c += GLA_V_WIDTH
    rg_ref[...] = proj(c, GLA_V_WIDTH).astype(BF16)
    c += GLA_V_WIDTH

    g_lr = proj(c, w_ref.shape[1] - c).astype(BF16)
    z = jnp.dot(g_lr, w2_ref[...], preferred_element_type=F32) + gb_ref[...]
    la_ref[...] = jax.nn.log_sigmoid(z) / GLA_TAU


def _inproj(x, gain, w_mix, w2, gate_b, cos, sin, layer):
    t, d = x.shape
    tm = PROJ_TM
    row = lambda width: pl.BlockSpec((tm, width), lambda i: (i, 0))
    widths = (ATTN_WIDTH, 2 * KV_WIDTH, 2 * KV_WIDTH, GLA_K_WIDTH, GLA_K_WIDTH, GLA_V_WIDTH,
              GLA_V_WIDTH, GLA_K_WIDTH)
    dtypes = (BF16, BF16, BF16, F32, F32, BF16, BF16, F32)
    return pl.pallas_call(
        _inproj_kernel,
        out_shape=[jax.ShapeDtypeStruct((t, w), dt) for w, dt in zip(widths, dtypes)],
        grid=(t // tm,),
        in_specs=[row(d), _resident((1, d)), _resident(w_mix.shape[1:], layer),
                  _resident(w2.shape[1:], layer), _resident((1, GLA_K_WIDTH)),
                  row(LANES), row(LANES)],
        out_specs=[row(w) for w in widths],
        compiler_params=_params("parallel"),
        name="mixer_inproj",
    )(x, gain, w_mix, w2, gate_b, cos, sin)


def _attn_kernel(sinks_ref, q_ref, kc_ref, vc_ref, kp_ref, vp_ref, o_ref):
    n = pl.program_id(1)
    blk = ATTN_BLOCK
    k_all = jnp.concatenate([kp_ref[...], kc_ref[...]], axis=0)
    v_all = jnp.concatenate([vp_ref[...], vc_ref[...]], axis=0)
    qi = lax.broadcasted_iota(jnp.int32, (2 * blk, blk), 0) % blk
    kj = lax.broadcasted_iota(jnp.int32, (2 * blk, blk), 1)
    top = lax.broadcasted_iota(jnp.int32, (2 * blk, 1), 0) < blk
    band_prev = kj > qi
    band_cur = kj <= qi
    band_prev_first = band_prev & (kj >= jnp.where(n > 0, 0, blk))
    lane = lax.broadcasted_iota(jnp.int32, (2 * blk, LANES), 1)
    key = lax.broadcasted_iota(jnp.int32, (2 * blk, LANES), 0)
    half = (lane < HEAD_DIM, lane >= HEAD_DIM)
    ones_half = tuple(jnp.where(hm, 1.0, 0.0).astype(BF16) for hm in half)
    zero = jnp.zeros((2 * blk, LANES), BF16)
    neg_inf = jnp.full((2 * blk, blk), -jnp.inf, F32)
    for g in range(N_KV_HEADS):
        gl = slice(g * LANES, (g + 1) * LANES)
        fills = []
        for parity in range(2):
            sink = jnp.where(top, sinks_ref[Q_PER_KV * g + parity], sinks_ref[Q_PER_KV * g + 2 + parity])
            fills.append(jnp.where(kj == 0, sink, neg_inf))
        for b in range(q_ref.shape[0] // blk):
            rows = slice(b * blk, (b + 1) * blk)
            q = jnp.concatenate([q_ref[rows, (2 * g) * LANES:(2 * g + 1) * LANES],
                                 q_ref[rows, (2 * g + 1) * LANES:(2 * g + 2) * LANES]], axis=0)
            kd = k_all[b * blk:(b + 2) * blk, gl]
            vd = v_all[b * blk:(b + 2) * blk, gl]
            mask_prev = band_prev_first if b == 0 else band_prev
            acc = None
            for parity in range(2):
                kz = jnp.where(half[parity], kd, zero)
                s = lax.dot_general(q, kz, (((1,), (1,)), ((), ())), preferred_element_type=F32)
                s_prev = jnp.where(mask_prev, s[:, :blk], fills[parity])
                s_cur = jnp.where(band_cur, s[:, blk:], neg_inf)
                m = jnp.max(jnp.maximum(s_prev, s_cur), axis=-1, keepdims=True)
                p = jnp.concatenate([jnp.exp(s_prev - m), jnp.exp(s_cur - m)], axis=1).astype(BF16)
                vz = jnp.where(half[parity] & (key > 0), vd, zero)
                w = jnp.concatenate([vz, ones_half[parity]], axis=1)
                part = jnp.dot(p, w, preferred_element_type=F32)
                acc = part if acc is None else acc + part
            out = (acc[:, :LANES] / acc[:, LANES:]).astype(BF16)
            o_ref[rows, (2 * g) * LANES:(2 * g + 1) * LANES] = out[:blk]
            o_ref[rows, (2 * g + 1) * LANES:(2 * g + 2) * LANES] = out[blk:]


def _attention(q, k, v, sinks, batch):
    t = q.shape[0]
    seq = t // batch
    tq = min(ATTN_TQ, seq)
    nq = seq // tq
    per = tq // ATTN_BLOCK
    kvw = k.shape[1]
    cur = lambda width: pl.BlockSpec((tq, width), lambda b, n: (b * nq + n, 0))
    prev = lambda width: pl.BlockSpec(
        (ATTN_BLOCK, width), lambda b, n: (jnp.maximum((b * nq + n) * per - 1, 0), 0))
    return pl.pallas_call(
        _attn_kernel,
        out_shape=jax.ShapeDtypeStruct((t, ATTN_WIDTH), BF16),
        grid=(batch, nq),
        in_specs=[pl.BlockSpec(memory_space=pltpu.SMEM),
                  cur(ATTN_WIDTH), cur(kvw), cur(kvw), prev(kvw), prev(kvw)],
        out_specs=cur(ATTN_WIDTH),
        compiler_params=_params("parallel", "arbitrary"),
        name="swa_attention",
    )(sinks, q, k, v, k, v)


def _gla_kernel(q_ref, k_ref, v_ref, la_ref, r_ref, gain_ref, o_ref, state_ref):
    @pl.when(pl.program_id(1) == 0)
    def _():
        state_ref[...] = jnp.zeros_like(state_ref)

    c = GLA_CHUNK
    pc = 2 * c
    ti = lax.broadcasted_iota(jnp.int32, (pc, pc), 0)
    si = lax.broadcasted_iota(jnp.int32, (pc, pc), 1)
    same_chunk = (ti >= c) == (si >= c)
    causal = same_chunk & (ti >= si)
    cross = (ti >= c) & (si < c)
    tri = causal.astype(F32)
    second = lax.broadcasted_iota(jnp.int32, (pc, 1), 0) >= c
    gain = gain_ref[...]
    scale = GLA_DK ** -0.5
    nt = (((1,), (1,)), ((), ()))
    state_t = [state_ref[h] for h in range(GLA_HEADS)]
    for pi in range(q_ref.shape[0] // pc):
        rows = slice(pi * pc, (pi + 1) * pc)
        la = la_ref[rows, :]
        b = jnp.dot(tri, la, precision=lax.Precision.HIGHEST, preferred_element_type=F32)
        b_last0 = b[c - 1:c, :]
        b_last1 = b[pc - 1:pc, :]
        decay0 = jnp.exp(b_last0)
        decay1 = jnp.exp(b_last1)
        q_dec = (q_ref[rows, :] * scale) * jnp.exp(b)
        k_all = k_ref[rows, :]
        k_dec = k_all * jnp.exp(-b)
        k_rem = k_all * jnp.exp(jnp.where(second, b_last1, b_last0) - b)
        q_in = jnp.where(second, q_dec * decay0, q_dec)
        k_out = jnp.where(second, k_rem, k_rem * decay1)
        for h in range(GLA_HEADS):
            kl = slice(h * GLA_DK, (h + 1) * GLA_DK)
            vl = slice(h * GLA_DV, (h + 1) * GLA_DV)
            qh = q_dec[:, kl].astype(BF16)
            qih = q_in[:, kl].astype(BF16)
            v_t = v_ref[rows, vl].astype(F32).T.astype(BF16)
            s2 = lax.dot_general(jnp.concatenate([qh, qih], axis=0), k_dec[:, kl].astype(BF16), nt,
                                 preferred_element_type=F32)
            intra = (jnp.where(causal, s2[:pc], 0.0) + jnp.where(cross, s2[pc:], 0.0)).astype(BF16)
            o = lax.dot_general(jnp.concatenate([qih, intra], axis=1),
                                jnp.concatenate([state_t[h].astype(BF16), v_t], axis=1),
                                nt, preferred_element_type=F32)
            d_state_t = jnp.dot(v_t, k_out[:, kl].astype(BF16), preferred_element_type=F32)
            state_t[h] = state_t[h] * (decay0[:, kl] * decay1[:, kl]) + d_state_t
            r = r_ref[rows, vl].astype(F32)
            o_ref[rows, vl] = (_rms(o, gain) * _silu(r)).astype(BF16)
    for h in range(GLA_HEADS):
        state_ref[h] = state_t[h]


def _gla(q, k, v, log_a, r, gain, batch):
    t = q.shape[0]
    seq = t // batch
    ts = min(GLA_TS, seq)
    ns = seq // ts
    row = lambda width: pl.BlockSpec((ts, width), lambda b, n: (b * ns + n, 0))
    return pl.pallas_call(
        _gla_kernel,
        out_shape=jax.ShapeDtypeStruct((t, GLA_V_WIDTH), BF16),
        grid=(batch, ns),
        in_specs=[row(GLA_K_WIDTH), row(GLA_K_WIDTH), row(GLA_V_WIDTH), row(GLA_K_WIDTH),
                  row(GLA_V_WIDTH), pl.BlockSpec((1, GLA_DV), lambda b, n: (0, 0))],
        out_specs=row(GLA_V_WIDTH),
        scratch_shapes=[pltpu.VMEM((GLA_HEADS, GLA_DV, GLA_DK), F32)],
        compiler_params=_params("parallel", "arbitrary"),
        name="gla",
    )(q, k, v, log_a, r, gain)


def _outproj_kernel(x_ref, a_ref, g_ref, w_ref, gain_ref, o_ref):
    part = o_ref.shape[0] // OUTPROJ_PARTS
    for r in range(OUTPROJ_PARTS):
        rows = slice(r * part, (r + 1) * part)
        h = (jnp.dot(a_ref[rows, :], w_ref[:ATTN_WIDTH, :], preferred_element_type=F32)
             + jnp.dot(g_ref[rows, :], w_ref[ATTN_WIDTH:, :], preferred_element_type=F32))
        o_ref[rows, :] = x_ref[rows, :] + _rms(h, gain_ref[...])


def _outproj(x, attn, gla, w_out, gain, layer):
    t, d = x.shape
    tm = PROJ_TM
    row = lambda width: pl.BlockSpec((tm, width), lambda i: (i, 0))
    return pl.pallas_call(
        _outproj_kernel,
        out_shape=jax.ShapeDtypeStruct((t, d), F32),
        grid=(t // tm,),
        in_specs=[row(d), row(ATTN_WIDTH), row(GLA_V_WIDTH), _resident(w_out.shape[1:], layer),
                  _resident((1, d))],
        out_specs=row(d),
        compiler_params=_params("parallel"),
        name="mixer_outproj",
    )(x, attn, gla, w_out, gain)


def _pad_to(a, axis, multiple):
    pad = (-a.shape[axis]) % multiple
    widths = [(0, 0)] * a.ndim
    widths[axis] = (0, pad)
    return jnp.pad(a, widths)


def _prep_ffn_weights(ffn_w_in, ffn_w_out):
    full = (ffn_w_out.shape[-2] // FFN_TF) * FFN_TF
    wgu, wgut = _cast_w_in(ffn_w_in)
    return wgu, _cast_w_out(ffn_w_out), wgut, ffn_w_out[..., full:, :].astype(BF16)


def _prep_mixer_weights(w_mix_in, gla_gate_w2, w_mix_out):
    w_mix = _pad_to(w_mix_in, -1, 2 * LANES).astype(BF16)
    main = w_mix_in.shape[-1] - GLA_GATE_RANK
    w2 = jnp.pad(gla_gate_w2, ((0, 0), (0, w_mix.shape[-1] - main - GLA_GATE_RANK), (0, 0))).astype(BF16)
    return w_mix, w2, w_mix_out.astype(BF16)


def kernel(x, positions, norm_gains, ffn_w_in, ffn_w_out, w_mix_in, attn_sinks, gla_gate_w2,
           gla_gate_b, gla_norm_gain, w_mix_out):
    batch, seq, d = x.shape
    depth = norm_gains.shape[0]
    xt = x.reshape(batch * seq, d)
    cos, sin = _rope_tables(positions)
    w_ffn = _prep_ffn_weights(ffn_w_in, ffn_w_out)
    w_mix, w2, w_mo = _prep_mixer_weights(w_mix_in, gla_gate_w2, w_mix_out)
    for l in range(depth):
        g = norm_gains[l].reshape(-1, 1, d)
        xt = _ffn(xt, g[0], g[1], w_ffn, l, 0)
        qa, ka, va, qg, kg, vg, rg, la = _inproj(
            xt, g[2], w_mix, w2, gla_gate_b[l].reshape(1, -1), cos, sin, l)
        attn = _attention(qa, ka, va, attn_sinks[l], batch)
        gla = _gla(qg, kg, vg, la, rg, gla_norm_gain[l].reshape(1, -1), batch)
        xt = _outproj(xt, attn, gla, w_mo, g[3], l)
        xt = _ffn(xt, g[4], g[5], w_ffn, l, 1)
    return xt.reshape(batch, seq, d)
```

```python
import functools

import numpy as np
import jax
import jax.numpy as jnp
from jax import lax
from jax.experimental import pallas as pl
from jax.experimental.pallas import tpu as pltpu

F32 = jnp.float32
BF16 = jnp.bfloat16

HEAD_DIM = 64
N_Q_HEADS = 16
N_KV_HEADS = 4
Q_PER_KV = N_Q_HEADS // N_KV_HEADS
ATTN_WIDTH = N_Q_HEADS * HEAD_DIM
KV_WIDTH = N_KV_HEADS * HEAD_DIM
ATTN_BLOCK = 128
ROPE_THETA = 10000.0
GLA_HEADS = 4
GLA_DK = 128
GLA_DV = 256
GLA_K_WIDTH = GLA_HEADS * GLA_DK
GLA_V_WIDTH = GLA_HEADS * GLA_DV
GLA_GATE_RANK = 16
GLA_TAU = 16.0
GLA_CHUNK = 64
NORM_EPS = 1e-6

LANES = 128
VMEM_LIMIT_BYTES = 60000 * 1024

FFN_TM = 1024
FFN_TF = 512
FFN_LAST_PARTS = 4
PROJ_TM = 512
OUTPROJ_PARTS = 1
ATTN_TQ = 512
GLA_TS = 512
CAST_ROWS = 256
CAST_OUT_STEPS = 4


def _params(*sem):
    return pltpu.CompilerParams(dimension_semantics=sem, vmem_limit_bytes=VMEM_LIMIT_BYTES)


def _resident(shape, lead=()):
    lead = tuple(lead) if isinstance(lead, (tuple, list)) else (lead,)
    return pl.BlockSpec((None,) * len(lead) + tuple(shape), lambda *_: lead + (0,) * len(shape),
                        pipeline_mode=pl.Buffered(1))


def _rms(x, gain, halve=False):
    ms = jnp.mean(x * x, axis=-1, keepdims=True)
    inv = lax.rsqrt(ms + NORM_EPS)
    if halve:
        inv = inv * 0.5
    return (x * inv) * gain


def _silu(x):
    return x * jax.nn.sigmoid(x)


def _swiglu_chunk(hn, wgu, wo):
    width = wo.shape[0]
    gu = jnp.dot(hn, wgu, preferred_element_type=F32)
    act = (_silu(gu[:, :width]) * gu[:, width:]).astype(BF16)
    return jnp.dot(act, wo, preferred_element_type=F32)


def _ffn_kernel(x_ref, gin_ref, gout_ref, wgu_ref, wo_ref, wgut_ref, wot_ref, o_ref, hn_ref):
    j = pl.program_id(1)
    last = pl.num_programs(1) - 1

    @pl.when(j == 0)
    def _():
        hn = _rms(x_ref[...], gin_ref[...]).astype(BF16)
        hn_ref[...] = hn
        o_ref[...] = _swiglu_chunk(hn, wgu_ref[...], wo_ref[...])

    @pl.when((j > 0) & (j < last))
    def _():
        o_ref[...] += _swiglu_chunk(hn_ref[...], wgu_ref[...], wo_ref[...])

    @pl.when(j == last)
    def _():
        part = o_ref.shape[0] // FFN_LAST_PARTS
        for r in range(FFN_LAST_PARTS):
            rows = slice(r * part, (r + 1) * part)
            h = o_ref[rows, :] + _swiglu_chunk(hn_ref[rows, :], wgut_ref[...], wot_ref[...])
            o_ref[rows, :] = x_ref[rows, :] + _rms(h, gout_ref[...], halve=True)


def _ffn(x, gin, gout, w, layer, slot):
    wgu, wo, wgut, wot = w
    t, d = x.shape
    n_full = wgu.shape[2]
    tail = wot.shape[2]
    assert n_full >= 2 and wo.shape[2] == n_full * FFN_TF + tail and 0 < tail and tail % LANES == 0
    main = lambda j: jnp.minimum(j, n_full - 1)
    return pl.pallas_call(
        _ffn_kernel,
        out_shape=jax.ShapeDtypeStruct((t, d), F32),
        grid=(t // FFN_TM, n_full + 1),
        in_specs=[
            pl.BlockSpec((FFN_TM, d), lambda i, j: (i, 0)),
            pl.BlockSpec((1, d), lambda i, j: (0, 0)),
            pl.BlockSpec((1, d), lambda i, j: (0, 0)),
            pl.BlockSpec((None, None, None, d, 2 * FFN_TF),
                         lambda i, j: (layer, slot, main(j), 0, 0)),
            pl.BlockSpec((None, None, FFN_TF, d), lambda i, j: (layer, slot, main(j), 0)),
            _resident((d, 2 * tail), (layer, slot)),
            _resident((tail, d), (layer, slot)),
        ],
        out_specs=pl.BlockSpec((FFN_TM, d), lambda i, j: (i, 0)),
        scratch_shapes=[pltpu.VMEM((FFN_TM, d), BF16)],
        compiler_params=_params("parallel", "arbitrary"),
        name="ffn",
    )(x, gin, gout, wgu, wo, wgut, wot)


def _cast_w_in_kernel(w_ref, main_ref, tail_ref):
    d_ff = w_ref.shape[1] // 2
    for c in range(main_ref.shape[0]):
        for half in range(2):
            src = half * d_ff + c * FFN_TF
            main_ref[c, :, half * FFN_TF:(half + 1) * FFN_TF] = w_ref[:, src:src + FFN_TF].astype(BF16)
    tail = tail_ref.shape[1] // 2
    for half in range(2):
        src = (half + 1) * d_ff - tail
        tail_ref[:, half * tail:(half + 1) * tail] = w_ref[:, src:src + tail].astype(BF16)


def _cast_w_in(ffn_w_in):
    depth, two, d, f2 = ffn_w_in.shape
    n_full = (f2 // 2) // FFN_TF
    tail = f2 // 2 - n_full * FFN_TF
    rt = CAST_ROWS
    pick = lambda m, i: (m // two, m % two)
    return pl.pallas_call(
        _cast_w_in_kernel,
        out_shape=[jax.ShapeDtypeStruct((depth, two, n_full, d, 2 * FFN_TF), BF16),
                   jax.ShapeDtypeStruct((depth, two, d, 2 * tail), BF16)],
        grid=(depth * two, d // rt),
        in_specs=[pl.BlockSpec((None, None, rt, f2), lambda m, i: pick(m, i) + (i, 0))],
        out_specs=[pl.BlockSpec((None, None, n_full, rt, 2 * FFN_TF), lambda m, i: pick(m, i) + (0, i, 0)),
                   pl.BlockSpec((None, None, rt, 2 * tail), lambda m, i: pick(m, i) + (i, 0))],
        compiler_params=_params("parallel", "parallel"),
        name="cast_ffn_w_in",
    )(ffn_w_in)


def _cast_rows_kernel(w_ref, o_ref):
    o_ref[...] = w_ref[...].astype(BF16)


def _cast_w_out(ffn_w_out):
    depth, two, f, d = ffn_w_out.shape
    rt = f // CAST_OUT_STEPS
    assert rt * CAST_OUT_STEPS == f and rt % 16 == 0
    spec = pl.BlockSpec((None, None, rt, d), lambda m, i: (m // two, m % two, i, 0))
    return pl.pallas_call(
        _cast_rows_kernel,
        out_shape=jax.ShapeDtypeStruct(ffn_w_out.shape, BF16),
        grid=(depth * two, CAST_OUT_STEPS),
        in_specs=[spec],
        out_specs=spec,
        compiler_params=_params("parallel", "parallel"),
        name="cast_ffn_w_out",
    )(ffn_w_out)


def _rope_table_kernel(pos_ref, invf_ref, cos_ref, sin_ref):
    ang = pos_ref[...].astype(F32) * invf_ref[...]
    lane = lax.broadcasted_iota(jnp.int32, ang.shape, 1)
    first_half = (lane % HEAD_DIM) < (HEAD_DIM // 2)
    cos_ref[...] = jnp.cos(ang)
    sin_ref[...] = jnp.where(first_half, -jnp.sin(ang), jnp.sin(ang))


def _rope_tables(positions):
    t = positions.size
    tm = min(t, 2048)
    inv_freq = ROPE_THETA ** (-jnp.arange(0, HEAD_DIM, 2, dtype=F32) / HEAD_DIM)
    invf = jnp.tile(inv_freq, LANES // (HEAD_DIM // 2)).reshape(1, LANES)
    return pl.pallas_call(
        _rope_table_kernel,
        out_shape=[jax.ShapeDtypeStruct((t, LANES), F32)] * 2,
        grid=(t // tm,),
        in_specs=[pl.BlockSpec((tm, 1), lambda i: (i, 0)),
                  pl.BlockSpec((1, LANES), lambda i: (0, 0))],
        out_specs=[pl.BlockSpec((tm, LANES), lambda i: (i, 0))] * 2,
        compiler_params=_params("parallel"),
        name="rope_tables",
    )(positions.reshape(t, 1), invf)


def _rope(x, cos, sin, first_half):
    half = HEAD_DIM // 2
    partner = jnp.where(first_half, pltpu.roll(x, LANES - half, axis=1), pltpu.roll(x, half, axis=1))
    return x * cos + partner * sin


def _inproj_kernel(x_ref, g_ref, w_ref, w2_ref, gb_ref, cos_ref, sin_ref,
                   qa_ref, ka_ref, va_ref, qg_ref, kg_ref, vg_ref, rg_ref, la_ref):
    hn = _rms(x_ref[...], g_ref[...]).astype(BF16)
    cos = cos_ref[...]
    sin = sin_ref[...]
    lane = lax.broadcasted_iota(jnp.int32, cos.shape, 1)
    first_half = (lane % HEAD_DIM) < (HEAD_DIM // 2)

    def proj(c0, width):
        return jnp.dot(hn, w_ref[:, c0:c0 + width], preferred_element_type=F32)

    def dup_into(dst_ref, s, slab):
        lo = lane < HEAD_DIM
        swapped = pltpu.roll(slab, HEAD_DIM, axis=1)
        dst_ref[:, (2 * s) * LANES:(2 * s + 1) * LANES] = jnp.where(lo, slab, swapped).astype(BF16)
        dst_ref[:, (2 * s + 1) * LANES:(2 * s + 2) * LANES] = jnp.where(lo, swapped, slab).astype(BF16)

    c_gate = ATTN_WIDTH + 2 * KV_WIDTH + 2 * GLA_K_WIDTH + 2 * GLA_V_WIDTH
    g_lr = proj(c_gate, w_ref.shape[1] - c_gate).astype(BF16)
    z = jnp.dot(g_lr, w2_ref[...], preferred_element_type=F32) + gb_ref[...]
    la_ref[...] = jax.nn.log_sigmoid(z) / GLA_TAU

    c = 0
    qa = proj(c, ATTN_WIDTH)
    for s in range(ATTN_WIDTH // LANES):
        cols = slice(s * LANES, (s + 1) * LANES)
        qa_ref[:, cols] = (_rope(qa[:, cols], cos, sin, first_half) * HEAD_DIM ** -0.5).astype(BF16)
    c += ATTN_WIDTH
    kv = proj(c, 2 * KV_WIDTH)
    for s in range(KV_WIDTH // LANES):
        dup_into(ka_ref, s, _rope(kv[:, s * LANES:(s + 1) * LANES], cos, sin, first_half))
        dup_into(va_ref, s, kv[:, KV_WIDTH + s * LANES:KV_WIDTH + (s + 1) * LANES])
    c += 2 * KV_WIDTH
    qg_ref[...] = proj(c, GLA_K_WIDTH)
    c += GLA_K_WIDTH
    kg_ref[...] = proj(c, GLA_K_WIDTH)
    c += GLA_K_WIDTH
    vg_ref[...] = proj(c, GLA_V_WIDTH).astype(BF16)
    c += GLA_V_WIDTH
    rg_ref[...] = proj(c, GLA_V_WIDTH).astype(BF16)
    assert c + GLA_V_WIDTH == c_gate


def _inproj(x, gain, w_mix, w2, gate_b, cos, sin, layer):
    t, d = x.shape
    tm = PROJ_TM
    row = lambda width: pl.BlockSpec((tm, width), lambda i: (i, 0))
    widths = (ATTN_WIDTH, 2 * KV_WIDTH, 2 * KV_WIDTH, GLA_K_WIDTH, GLA_K_WIDTH, GLA_V_WIDTH,
              GLA_V_WIDTH, GLA_K_WIDTH)
    dtypes = (BF16, BF16, BF16, F32, F32, BF16, BF16, F32)
    return pl.pallas_call(
        _inproj_kernel,
        out_shape=[jax.ShapeDtypeStruct((t, w), dt) for w, dt in zip(widths, dtypes)],
        grid=(t // tm,),
        in_specs=[row(d), _resident((1, d)), _resident(w_mix.shape[1:], layer),
                  _resident(w2.shape[1:], layer), _resident((1, GLA_K_WIDTH)),
                  row(LANES), row(LANES)],
        out_specs=[row(w) for w in widths],
        compiler_params=_params("parallel"),
        name="mixer_inproj",
    )(x, gain, w_mix, w2, gate_b, cos, sin)


def _attn_kernel(sinks_ref, q_ref, kc_ref, vc_ref, kp_ref, vp_ref, o_ref):
    n = pl.program_id(1)
    blk = ATTN_BLOCK
    k_all = jnp.concatenate([kp_ref[...], kc_ref[...]], axis=0)
    v_all = jnp.concatenate([vp_ref[...], vc_ref[...]], axis=0)
    qi = lax.broadcasted_iota(jnp.int32, (2 * blk, blk), 0) % blk
    kj = lax.broadcasted_iota(jnp.int32, (2 * blk, blk), 1)
    top = lax.broadcasted_iota(jnp.int32, (2 * blk, 1), 0) < blk
    band_prev = kj > qi
    band_cur = kj <= qi
    band_prev_first = band_prev & (kj >= jnp.where(n > 0, 0, blk))
    lane = lax.broadcasted_iota(jnp.int32, (2 * blk, LANES), 1)
    key = lax.broadcasted_iota(jnp.int32, (2 * blk, LANES), 0)
    half = (lane < HEAD_DIM, lane >= HEAD_DIM)
    ones_half = tuple(jnp.where(hm, 1.0, 0.0).astype(BF16) for hm in half)
    zero = jnp.zeros((2 * blk, LANES), BF16)
    neg_inf = jnp.full((2 * blk, blk), -jnp.inf, F32)
    for g in range(N_KV_HEADS):
        gl = slice(g * LANES, (g + 1) * LANES)
        fills = []
        for parity in range(2):
            sink = jnp.where(top, sinks_ref[Q_PER_KV * g + parity], sinks_ref[Q_PER_KV * g + 2 + parity])
            fills.append(jnp.where(kj == 0, sink, neg_inf))
        for b in range(q_ref.shape[0] // blk):
            rows = slice(b * blk, (b + 1) * blk)
            q = jnp.concatenate([q_ref[rows, (2 * g) * LANES:(2 * g + 1) * LANES],
                                 q_ref[rows, (2 * g + 1) * LANES:(2 * g + 2) * LANES]], axis=0)
            kd = k_all[b * blk:(b + 2) * blk, gl]
            vd = v_all[b * blk:(b + 2) * blk, gl]
            mask_prev = band_prev_first if b == 0 else band_prev
            acc = None
            for parity in range(2):
                kz = jnp.where(half[parity], kd, zero)
                s = lax.dot_general(q, kz, (((1,), (1,)), ((), ())), preferred_element_type=F32)
                s_prev = jnp.where(mask_prev, s[:, :blk], fills[parity])
                s_cur = jnp.where(band_cur, s[:, blk:], neg_inf)
                m = jnp.max(jnp.maximum(s_prev, s_cur), axis=-1, keepdims=True)
                p = jnp.concatenate([jnp.exp(s_prev - m), jnp.exp(s_cur - m)], axis=1).astype(BF16)
                vz = jnp.where(half[parity] & (key > 0), vd, zero)
                w = jnp.concatenate([vz, ones_half[parity]], axis=1)
                part = jnp.dot(p, w, preferred_element_type=F32)
                acc = part if acc is None else acc + part
            out = (acc[:, :LANES] / acc[:, LANES:]).astype(BF16)
            o_ref[rows, (2 * g) * LANES:(2 * g + 1) * LANES] = out[:blk]
            o_ref[rows, (2 * g + 1) * LANES:(2 * g + 2) * LANES] = out[blk:]


def _attention(q, k, v, sinks, batch):
    t = q.shape[0]
    seq = t // batch
    tq = min(ATTN_TQ, seq)
    nq = seq // tq
    per = tq // ATTN_BLOCK
    kvw = k.shape[1]
    cur = lambda width: pl.BlockSpec((tq, width), lambda b, n: (b * nq + n, 0))
    prev = lambda width: pl.BlockSpec(
        (ATTN_BLOCK, width), lambda b, n: (jnp.maximum((b * nq + n) * per - 1, 0), 0))
    return pl.pallas_call(
        _attn_kernel,
        out_shape=jax.ShapeDtypeStruct((t, ATTN_WIDTH), BF16),
        grid=(batch, nq),
        in_specs=[pl.BlockSpec(memory_space=pltpu.SMEM),
                  cur(ATTN_WIDTH), cur(kvw), cur(kvw), prev(kvw), prev(kvw)],
        out_specs=cur(ATTN_WIDTH),
        compiler_params=_params("parallel", "arbitrary"),
        name="swa_attention",
    )(sinks, q, k, v, k, v)


def _gla_kernel(q_ref, k_ref, v_ref, la_ref, r_ref, gain_ref, o_ref, state_ref):
    @pl.when(pl.program_id(1) == 0)
    def _():
        state_ref[...] = jnp.zeros_like(state_ref)

    c = GLA_CHUNK
    pc = 2 * c
    ti = lax.broadcasted_iota(jnp.int32, (pc, pc), 0)
    si = lax.broadcasted_iota(jnp.int32, (pc, pc), 1)
    same_chunk = (ti >= c) == (si >= c)
    causal = same_chunk & (ti >= si)
    cross = (ti >= c) & (si < c)
    tri = causal.astype(F32)
    second = lax.broadcasted_iota(jnp.int32, (pc, 1), 0) >= c
    gain = gain_ref[...]
    scale = GLA_DK ** -0.5
    nt = (((1,), (1,)), ((), ()))
    state_t = [state_ref[h] for h in range(GLA_HEADS)]
    for pi in range(q_ref.shape[0] // pc):
        rows = slice(pi * pc, (pi + 1) * pc)
        la = la_ref[rows, :]
        b = jnp.dot(tri, la, precision=lax.Precision.HIGHEST, preferred_element_type=F32)
        b_last0 = b[c - 1:c, :]
        b_last1 = b[pc - 1:pc, :]
        decay0 = jnp.exp(b_last0)
        decay1 = jnp.exp(b_last1)
        q_dec = (q_ref[rows, :] * scale) * jnp.exp(b)
        k_all = k_ref[rows, :]
        k_dec = k_all * jnp.exp(-b)
        k_rem = k_all * jnp.exp(jnp.where(second, b_last1, b_last0) - b)
        q_in = jnp.where(second, q_dec * decay0, q_dec)
        k_out = jnp.where(second, k_rem, k_rem * decay1)
        for h in range(GLA_HEADS):
            kl = slice(h * GLA_DK, (h + 1) * GLA_DK)
            vl = slice(h * GLA_DV, (h + 1) * GLA_DV)
            qh = q_dec[:, kl].astype(BF16)
            qih = q_in[:, kl].astype(BF16)
            v_t = v_ref[rows, vl].astype(F32).T.astype(BF16)
            s2 = lax.dot_general(jnp.concatenate([qh, qih], axis=0), k_dec[:, kl].astype(BF16), nt,
                                 preferred_element_type=F32)
            intra = (jnp.where(causal, s2[:pc], 0.0) + jnp.where(cross, s2[pc:], 0.0)).astype(BF16)
            o = lax.dot_general(jnp.concatenate([qih, intra], axis=1),
                                jnp.concatenate([state_t[h].astype(BF16), v_t], axis=1),
                                nt, preferred_element_type=F32)
            d_state_t = jnp.dot(v_t, k_out[:, kl].astype(BF16), preferred_element_type=F32)
            state_t[h] = state_t[h] * (decay0[:, kl] * decay1[:, kl]) + d_state_t
            r = r_ref[rows, vl].astype(F32)
            o_ref[rows, vl] = (_rms(o, gain) * _silu(r)).astype(BF16)
    for h in range(GLA_HEADS):
        state_ref[h] = state_t[h]


def _gla(q, k, v, log_a, r, gain, batch):
    t = q.shape[0]
    seq = t // batch
    ts = min(GLA_TS, seq)
    ns = seq // ts
    row = lambda width: pl.BlockSpec((ts, width), lambda b, n: (b * ns + n, 0))
    return pl.pallas_call(
        _gla_kernel,
        out_shape=jax.ShapeDtypeStruct((t, GLA_V_WIDTH), BF16),
        grid=(batch, ns),
        in_specs=[row(GLA_K_WIDTH), row(GLA_K_WIDTH), row(GLA_V_WIDTH), row(GLA_K_WIDTH),
                  row(GLA_V_WIDTH), pl.BlockSpec((1, GLA_DV), lambda b, n: (0, 0))],
        out_specs=row(GLA_V_WIDTH),
        scratch_shapes=[pltpu.VMEM((GLA_HEADS, GLA_DV, GLA_DK), F32)],
        compiler_params=_params("parallel", "arbitrary"),
        name="gla",
    )(q, k, v, log_a, r, gain)


def _outproj_kernel(x_ref, a_ref, g_ref, w_ref, gain_ref, o_ref):
    part = o_ref.shape[0] // OUTPROJ_PARTS
    for r in range(OUTPROJ_PARTS):
        rows = slice(r * part, (r + 1) * part)
        h = (jnp.dot(a_ref[rows, :], w_ref[:ATTN_WIDTH, :], preferred_element_type=F32)
             + jnp.dot(g_ref[rows, :], w_ref[ATTN_WIDTH:, :], preferred_element_type=F32))
        o_ref[rows, :] = x_ref[rows, :] + _rms(h, gain_ref[...])


def _outproj(x, attn, gla, w_out, gain, layer):
    t, d = x.shape
    tm = PROJ_TM
    row = lambda width: pl.BlockSpec((tm, width), lambda i: (i, 0))
    return pl.pallas_call(
        _outproj_kernel,
        out_shape=jax.ShapeDtypeStruct((t, d), F32),
        grid=(t // tm,),
        in_specs=[row(d), row(ATTN_WIDTH), row(GLA_V_WIDTH), _resident(w_out.shape[1:], layer),
                  _resident((1, d))],
        out_specs=row(d),
        compiler_params=_params("parallel"),
        name="mixer_outproj",
    )(x, attn, gla, w_out, gain)


def _pad_to(a, axis, multiple):
    pad = (-a.shape[axis]) % multiple
    widths = [(0, 0)] * a.ndim
    widths[axis] = (0, pad)
    return jnp.pad(a, widths)


def _prep_ffn_weights(ffn_w_in, ffn_w_out):
    full = (ffn_w_out.shape[-2] // FFN_TF) * FFN_TF
    wgu, wgut = _cast_w_in(ffn_w_in)
    return wgu, _cast_w_out(ffn_w_out), wgut, ffn_w_out[..., full:, :].astype(BF16)


def _prep_mixer_weights(w_mix_in, gla_gate_w2, w_mix_out):
    w_mix = _pad_to(w_mix_in, -1, 2 * LANES).astype(BF16)
    main = w_mix_in.shape[-1] - GLA_GATE_RANK
    w2 = jnp.pad(gla_gate_w2, ((0, 0), (0, w_mix.shape[-1] - main - GLA_GATE_RANK), (0, 0))).astype(BF16)
    return w_mix, w2, w_mix_out.astype(BF16)


def kernel(x, positions, norm_gains, ffn_w_in, ffn_w_out, w_mix_in, attn_sinks, gla_gate_w2,
           gla_gate_b, gla_norm_gain, w_mix_out):
    batch, seq, d = x.shape
    depth = norm_gains.shape[0]
    xt = x.reshape(batch * seq, d)
    cos, sin = _rope_tables(positions)
    w_ffn = _prep_ffn_weights(ffn_w_in, ffn_w_out)
    w_mix, w2, w_mo = _prep_mixer_weights(w_mix_in, gla_gate_w2, w_mix_out)
    for l in range(depth):
        g = norm_gains[l].reshape(-1, 1, d)
        xt = _ffn(xt, g[0], g[1], w_ffn, l, 0)
        qa, ka, va, qg, kg, vg, rg, la = _inproj(
            xt, g[2], w_mix, w2, gla_gate_b[l].reshape(1, -1), cos, sin, l)
        attn = _attention(qa, ka, va, attn_sinks[l], batch)
        gla = _gla(qg, kg, vg, la, rg, gla_norm_gain[l].reshape(1, -1), batch)
        xt = _outproj(xt, attn, gla, w_mo, g[3], l)
        xt = _ffn(xt, g[4], g[5], w_ffn, l, 1)
    return xt.reshape(batch, seq, d)
```

```python
import jax
import jax.numpy as jnp
from jax import lax
from jax.experimental import pallas as pl
from jax.experimental.pallas import tpu as pltpu

F32 = jnp.float32
BF16 = jnp.bfloat16

HEAD_DIM = 64
N_Q_HEADS = 16
N_KV_HEADS = 4
Q_PER_KV = N_Q_HEADS // N_KV_HEADS
ATTN_WIDTH = N_Q_HEADS * HEAD_DIM
KV_WIDTH = N_KV_HEADS * HEAD_DIM
ATTN_BLOCK = 128
ROPE_THETA = 10000.0
GLA_HEADS = 4
GLA_DK = 128
GLA_DV = 256
GLA_K_WIDTH = GLA_HEADS * GLA_DK
GLA_V_WIDTH = GLA_HEADS * GLA_DV
GLA_GATE_RANK = 16
GLA_TAU = 16.0
GLA_CHUNK = 64
NORM_EPS = 1e-6

LANES = 128
VMEM_LIMIT_BYTES = 60000 * 1024

FFN_TM = 1024
FFN_TF = 512
FFN_LAST_PARTS = 4
PROJ_TM = 512
OUTPROJ_PARTS = 1
ATTN_TQ = 512
GLA_TS = 512
CAST_ROWS = 256
CAST_OUT_STEPS = 4


def _params(*sem):
    return pltpu.CompilerParams(dimension_semantics=sem, vmem_limit_bytes=VMEM_LIMIT_BYTES)


def _resident(shape, lead=()):
    lead = tuple(lead) if isinstance(lead, (tuple, list)) else (lead,)
    return pl.BlockSpec((None,) * len(lead) + tuple(shape), lambda *_: lead + (0,) * len(shape),
                        pipeline_mode=pl.Buffered(1))


def _rms(x, gain, halve=False):
    ms = jnp.mean(x * x, axis=-1, keepdims=True)
    inv = lax.rsqrt(ms + NORM_EPS)
    if halve:
        inv = inv * 0.5
    return (x * inv) * gain


def _silu(x):
    return x * jax.nn.sigmoid(x)


def _swiglu_chunk(hn, wgu, wo):
    width = wo.shape[0]
    gu = jnp.dot(hn, wgu, preferred_element_type=F32)
    act = (_silu(gu[:, :width]) * gu[:, width:]).astype(BF16)
    return jnp.dot(act, wo, preferred_element_type=F32)


def _ffn_kernel(x_ref, gin_ref, gout_ref, wgu_ref, wo_ref, wgut_ref, wot_ref, o_ref, hn_ref):
    j = pl.program_id(1)
    last = pl.num_programs(1) - 1

    @pl.when(j == 0)
    def _():
        hn = _rms(x_ref[...], gin_ref[...]).astype(BF16)
        hn_ref[...] = hn
        o_ref[...] = _swiglu_chunk(hn, wgu_ref[...], wo_ref[...])

    @pl.when((j > 0) & (j < last))
    def _():
        o_ref[...] += _swiglu_chunk(hn_ref[...], wgu_ref[...], wo_ref[...])

    @pl.when(j == last)
    def _():
        part = o_ref.shape[0] // FFN_LAST_PARTS
        for r in range(FFN_LAST_PARTS):
            rows = slice(r * part, (r + 1) * part)
            h = o_ref[rows, :] + _swiglu_chunk(hn_ref[rows, :], wgut_ref[...], wot_ref[...])
            o_ref[rows, :] = x_ref[rows, :] + _rms(h, gout_ref[...], halve=True)


def _ffn(x, gin, gout, w, layer, slot):
    wgu, wo, wgut, wot = w
    t, d = x.shape
    n_full = wgu.shape[2]
    tail = wot.shape[2]
    assert n_full >= 2 and wo.shape[2] == n_full * FFN_TF + tail and 0 < tail and tail % LANES == 0
    main = lambda j: jnp.minimum(j, n_full - 1)
    return pl.pallas_call(
        _ffn_kernel,
        out_shape=jax.ShapeDtypeStruct((t, d), F32),
        grid=(t // FFN_TM, n_full + 1),
        in_specs=[
            pl.BlockSpec((FFN_TM, d), lambda i, j: (i, 0)),
            pl.BlockSpec((1, d), lambda i, j: (0, 0)),
            pl.BlockSpec((1, d), lambda i, j: (0, 0)),
            pl.BlockSpec((None, None, None, d, 2 * FFN_TF),
                         lambda i, j: (layer, slot, main(j), 0, 0)),
            pl.BlockSpec((None, None, FFN_TF, d), lambda i, j: (layer, slot, main(j), 0)),
            _resident((d, 2 * tail), (layer, slot)),
            _resident((tail, d), (layer, slot)),
        ],
        out_specs=pl.BlockSpec((FFN_TM, d), lambda i, j: (i, 0)),
        scratch_shapes=[pltpu.VMEM((FFN_TM, d), BF16)],
        compiler_params=_params("parallel", "arbitrary"),
        name="ffn",
    )(x, gin, gout, wgu, wo, wgut, wot)


def _cast_w_in_kernel(w_ref, main_ref, tail_ref):
    d_ff = w_ref.shape[1] // 2
    for c in range(main_ref.shape[0]):
        for half in range(2):
            src = half * d_ff + c * FFN_TF
            main_ref[c, :, half * FFN_TF:(half + 1) * FFN_TF] = w_ref[:, src:src + FFN_TF].astype(BF16)
    tail = tail_ref.shape[1] // 2
    for half in range(2):
        src = (half + 1) * d_ff - tail
        tail_ref[:, half * tail:(half + 1) * tail] = w_ref[:, src:src + tail].astype(BF16)


def _cast_w_in(ffn_w_in):
    depth, two, d, f2 = ffn_w_in.shape
    n_full = (f2 // 2) // FFN_TF
    tail = f2 // 2 - n_full * FFN_TF
    rt = CAST_ROWS
    pick = lambda m, i: (m // two, m % two)
    return pl.pallas_call(
        _cast_w_in_kernel,
        out_shape=[jax.ShapeDtypeStruct((depth, two, n_full, d, 2 * FFN_TF), BF16),
                   jax.ShapeDtypeStruct((depth, two, d, 2 * tail), BF16)],
        grid=(depth * two, d // rt),
        in_specs=[pl.BlockSpec((None, None, rt, f2), lambda m, i: pick(m, i) + (i, 0))],
        out_specs=[pl.BlockSpec((None, None, n_full, rt, 2 * FFN_TF), lambda m, i: pick(m, i) + (0, i, 0)),
                   pl.BlockSpec((None, None, rt, 2 * tail), lambda m, i: pick(m, i) + (i, 0))],
        compiler_params=_params("parallel", "parallel"),
        name="cast_ffn_w_in",
    )(ffn_w_in)


def _cast_rows_kernel(w_ref, o_ref):
    o_ref[...] = w_ref[...].astype(BF16)


def _cast_w_out(ffn_w_out):
    depth, two, f, d = ffn_w_out.shape
    rt = f // CAST_OUT_STEPS
    assert rt * CAST_OUT_STEPS == f and rt % 16 == 0
    spec = pl.BlockSpec((None, None, rt, d), lambda m, i: (m // two, m % two, i, 0))
    return pl.pallas_call(
        _cast_rows_kernel,
        out_shape=jax.ShapeDtypeStruct(ffn_w_out.shape, BF16),
        grid=(depth * two, CAST_OUT_STEPS),
        in_specs=[spec],
        out_specs=spec,
        compiler_params=_params("parallel", "parallel"),
        name="cast_ffn_w_out",
    )(ffn_w_out)


def _rope_table_kernel(pos_ref, invf_ref, cos_ref, sin_ref):
    ang = pos_ref[...].astype(F32) * invf_ref[...]
    lane = lax.broadcasted_iota(jnp.int32, ang.shape, 1)
    first_half = (lane % HEAD_DIM) < (HEAD_DIM // 2)
    cos_ref[...] = jnp.cos(ang)
    sin_ref[...] = jnp.where(first_half, -jnp.sin(ang), jnp.sin(ang))


def _rope_tables(positions):
    t = positions.size
    tm = min(t, 2048)
    inv_freq = ROPE_THETA ** (-jnp.arange(0, HEAD_DIM, 2, dtype=F32) / HEAD_DIM)
    invf = jnp.tile(inv_freq, LANES // (HEAD_DIM // 2)).reshape(1, LANES)
    return pl.pallas_call(
        _rope_table_kernel,
        out_shape=[jax.ShapeDtypeStruct((t, LANES), F32)] * 2,
        grid=(t // tm,),
        in_specs=[pl.BlockSpec((tm, 1), lambda i: (i, 0)),
                  pl.BlockSpec((1, LANES), lambda i: (0, 0))],
        out_specs=[pl.BlockSpec((tm, LANES), lambda i: (i, 0))] * 2,
        compiler_params=_params("parallel"),
        name="rope_tables",
    )(positions.reshape(t, 1), invf)


def _rope(x, cos, sin, first_half):
    half = HEAD_DIM // 2
    partner = jnp.where(first_half, pltpu.roll(x, LANES - half, axis=1), pltpu.roll(x, half, axis=1))
    return x * cos + partner * sin


def _inproj_kernel(x_ref, g_ref, w_ref, w2_ref, gb_ref, cos_ref, sin_ref,
                   qa_ref, ka_ref, va_ref, qg_ref, kg_ref, vg_ref, rg_ref, la_ref):
    hn = _rms(x_ref[...], g_ref[...]).astype(BF16)
    cos = cos_ref[...]
    sin = sin_ref[...]
    lane = lax.broadcasted_iota(jnp.int32, cos.shape, 1)
    first_half = (lane % HEAD_DIM) < (HEAD_DIM // 2)

    def proj(c0, width):
        return jnp.dot(hn, w_ref[:, c0:c0 + width], preferred_element_type=F32)

    def dup_into(dst_ref, s, slab):
        lo = lane < HEAD_DIM
        swapped = pltpu.roll(slab, HEAD_DIM, axis=1)
        dst_ref[:, (2 * s) * LANES:(2 * s + 1) * LANES] = jnp.where(lo, slab, swapped).astype(BF16)
        dst_ref[:, (2 * s + 1) * LANES:(2 * s + 2) * LANES] = jnp.where(lo, swapped, slab).astype(BF16)

    c_gate = ATTN_WIDTH + 2 * KV_WIDTH + 2 * GLA_K_WIDTH + 2 * GLA_V_WIDTH
    g_lr = proj(c_gate, w_ref.shape[1] - c_gate).astype(BF16)
    z = jnp.dot(g_lr, w2_ref[...], preferred_element_type=F32) + gb_ref[...]
    la_ref[...] = jax.nn.log_sigmoid(z) / GLA_TAU

    c = 0
    qa = proj(c, ATTN_WIDTH)
    for s in range(ATTN_WIDTH // LANES):
        cols = slice(s * LANES, (s + 1) * LANES)
        qa_ref[:, cols] = (_rope(qa[:, cols], cos, sin, first_half) * HEAD_DIM ** -0.5).astype(BF16)
    c += ATTN_WIDTH
    kv = proj(c, 2 * KV_WIDTH)
    for s in range(KV_WIDTH // LANES):
        dup_into(ka_ref, s, _rope(kv[:, s * LANES:(s + 1) * LANES], cos, sin, first_half))
        dup_into(va_ref, s, kv[:, KV_WIDTH + s * LANES:KV_WIDTH + (s + 1) * LANES])
    c += 2 * KV_WIDTH
    qg_ref[...] = proj(c, GLA_K_WIDTH)
    c += GLA_K_WIDTH
    kg_ref[...] = proj(c, GLA_K_WIDTH)
    c += GLA_K_WIDTH
    vg_ref[...] = proj(c, GLA_V_WIDTH).astype(BF16)
    c += GLA_V_WIDTH
    rg_ref[...] = proj(c, GLA_V_WIDTH).astype(BF16)
    assert c + GLA_V_WIDTH == c_gate


def _inproj(x, gain, w_mix, w2, gate_b, cos, sin, layer):
    t, d = x.shape
    tm = PROJ_TM
    row = lambda width: pl.BlockSpec((tm, width), lambda i: (i, 0))
    widths = (ATTN_WIDTH, 2 * KV_WIDTH, 2 * KV_WIDTH, GLA_K_WIDTH, GLA_K_WIDTH, GLA_V_WIDTH,
              GLA_V_WIDTH, GLA_K_WIDTH)
    dtypes = (BF16, BF16, BF16, F32, F32, BF16, BF16, F32)
    return pl.pallas_call(
        _inproj_kernel,
        out_shape=[jax.ShapeDtypeStruct((t, w), dt) for w, dt in zip(widths, dtypes)],
        grid=(t // tm,),
        in_specs=[row(d), _resident((1, d)), _resident(w_mix.shape[1:], layer),
                  _resident(w2.shape[1:], layer), _resident((1, GLA_K_WIDTH)),
                  row(LANES), row(LANES)],
        out_specs=[row(w) for w in widths],
        compiler_params=_params("parallel"),
        name="mixer_inproj",
    )(x, gain, w_mix, w2, gate_b, cos, sin)


def _attn_kernel(sinks_ref, q_ref, kc_ref, vc_ref, kp_ref, vp_ref, o_ref):
    n = pl.program_id(1)
    blk = ATTN_BLOCK
    k_all = jnp.concatenate([kp_ref[...], kc_ref[...]], axis=0)
    v_all = jnp.concatenate([vp_ref[...], vc_ref[...]], axis=0)
    qi = lax.broadcasted_iota(jnp.int32, (2 * blk, blk), 0) % blk
    kj = lax.broadcasted_iota(jnp.int32, (2 * blk, blk), 1)
    top = lax.broadcasted_iota(jnp.int32, (2 * blk, 1), 0) < blk
    band_prev = kj > qi
    band_cur = kj <= qi
    band_prev_first = band_prev & (kj >= jnp.where(n > 0, 0, blk))
    lane = lax.broadcasted_iota(jnp.int32, (2 * blk, LANES), 1)
    key = lax.broadcasted_iota(jnp.int32, (2 * blk, LANES), 0)
    half = (lane < HEAD_DIM, lane >= HEAD_DIM)
    ones_half = tuple(jnp.where(hm, 1.0, 0.0).astype(BF16) for hm in half)
    zero = jnp.zeros((2 * blk, LANES), BF16)
    neg_inf = jnp.full((2 * blk, blk), -jnp.inf, F32)
    for g in range(N_KV_HEADS):
        gl = slice(g * LANES, (g + 1) * LANES)
        fills = []
        for parity in range(2):
            sink = jnp.where(top, sinks_ref[Q_PER_KV * g + parity], sinks_ref[Q_PER_KV * g + 2 + parity])
            fills.append(jnp.where(kj == 0, sink, neg_inf))
        for b in range(q_ref.shape[0] // blk):
            rows = slice(b * blk, (b + 1) * blk)
            q = jnp.concatenate([q_ref[rows, (2 * g) * LANES:(2 * g + 1) * LANES],
                                 q_ref[rows, (2 * g + 1) * LANES:(2 * g + 2) * LANES]], axis=0)
            kd = k_all[b * blk:(b + 2) * blk, gl]
            vd = v_all[b * blk:(b + 2) * blk, gl]
            mask_prev = band_prev_first if b == 0 else band_prev
            acc = None
            for parity in range(2):
                kz = jnp.where(half[parity], kd, zero)
                s = lax.dot_general(q, kz, (((1,), (1,)), ((), ())), preferred_element_type=F32)
                s_prev = jnp.where(mask_prev, s[:, :blk], fills[parity])
                s_cur = jnp.where(band_cur, s[:, blk:], neg_inf)
                m = jnp.max(jnp.maximum(s_prev, s_cur), axis=-1, keepdims=True)
                p = jnp.concatenate([jnp.exp(s_prev - m), jnp.exp(s_cur - m)], axis=1).astype(BF16)
                vz = jnp.where(half[parity] & (key > 0), vd, zero)
                w = jnp.concatenate([vz, ones_half[parity]], axis=1)
                part = jnp.dot(p, w, preferred_element_type=F32)
                acc = part if acc is None else acc + part
            out = (acc[:, :LANES] / acc[:, LANES:]).astype(BF16)
            o_ref[rows, (2 * g) * LANES:(2 * g + 1) * LANES] = out[:blk]
            o_ref[rows, (2 * g + 1) * LANES:(2 * g + 2) * LANES] = out[blk:]


def _attention(q, k, v, sinks, batch):
    t = q.shape[0]
    seq = t // batch
    tq = min(ATTN_TQ, seq)
    nq = seq // tq
    per = tq // ATTN_BLOCK
    kvw = k.shape[1]
    cur = lambda width: pl.BlockSpec((tq, width), lambda b, n: (b * nq + n, 0))
    prev = lambda width: pl.BlockSpec(
        (ATTN_BLOCK, width), lambda b, n: (jnp.maximum((b * nq + n) * per - 1, 0), 0))
    return pl.pallas_call(
        _attn_kernel,
        out_shape=jax.ShapeDtypeStruct((t, ATTN_WIDTH), BF16),
        grid=(batch, nq),
        in_specs=[pl.BlockSpec(memory_space=pltpu.SMEM),
                  cur(ATTN_WIDTH), cur(kvw), cur(kvw), prev(kvw), prev(kvw)],
        out_specs=cur(ATTN_WIDTH),
        compiler_params=_params("parallel", "arbitrary"),
        name="swa_attention",
    )(sinks, q, k, v, k, v)


def _gla_kernel(q_ref, k_ref, v_ref, la_ref, r_ref, gain_ref, o_ref, state_ref):
    @pl.when(pl.program_id(1) == 0)
    def _():
        state_ref[...] = jnp.zeros_like(state_ref)

    c = GLA_CHUNK
    pc = 2 * c
    ti = lax.broadcasted_iota(jnp.int32, (pc, pc), 0)
    si = lax.broadcasted_iota(jnp.int32, (pc, pc), 1)
    same_chunk = (ti >= c) == (si >= c)
    causal = same_chunk & (ti >= si)
    cross = (ti >= c) & (si < c)
    tri = causal.astype(F32)
    second = lax.broadcasted_iota(jnp.int32, (pc, 1), 0) >= c
    gain = gain_ref[...]
    scale = GLA_DK ** -0.5
    nt = (((1,), (1,)), ((), ()))
    state_t = [state_ref[h] for h in range(GLA_HEADS)]
    for pi in range(q_ref.shape[0] // pc):
        rows = slice(pi * pc, (pi + 1) * pc)
        la = la_ref[rows, :]
        b = jnp.dot(tri, la, precision=lax.Precision.HIGHEST, preferred_element_type=F32)
        b_last0 = b[c - 1:c, :]
        b_last1 = b[pc - 1:pc, :]
        decay0 = jnp.exp(b_last0)
        decay1 = jnp.exp(b_last1)
        q_dec = (q_ref[rows, :] * scale) * jnp.exp(b)
        k_all = k_ref[rows, :]
        k_dec = k_all * jnp.exp(-b)
        k_rem = k_all * jnp.exp(jnp.where(second, b_last1, b_last0) - b)
        q_in = jnp.where(second, q_dec * decay0, q_dec)
        k_out = jnp.where(second, k_rem, k_rem * decay1)
        for h in range(GLA_HEADS):
            kl = slice(h * GLA_DK, (h + 1) * GLA_DK)
            vl = slice(h * GLA_DV, (h + 1) * GLA_DV)
            qh = q_dec[:, kl].astype(BF16)
            qih = q_in[:, kl].astype(BF16)
            v_t = v_ref[rows, vl].astype(F32).T.astype(BF16)
            s2 = lax.dot_general(jnp.concatenate([qh, qih], axis=0), k_dec[:, kl].astype(BF16), nt,
                                 preferred_element_type=F32)
            intra = (jnp.where(causal, s2[:pc], 0.0) + jnp.where(cross, s2[pc:], 0.0)).astype(BF16)
            o = lax.dot_general(jnp.concatenate([qih, intra], axis=1),
                                jnp.concatenate([state_t[h].astype(BF16), v_t], axis=1),
                                nt, preferred_element_type=F32)
            d_state_t = jnp.dot(v_t, k_out[:, kl].astype(BF16), preferred_element_type=F32)
            state_t[h] = state_t[h] * (decay0[:, kl] * decay1[:, kl]) + d_state_t
            r = r_ref[rows, vl].astype(F32)
            o_ref[rows, vl] = (_rms(o, gain) * _silu(r)).astype(BF16)
    for h in range(GLA_HEADS):
        state_ref[h] = state_t[h]


def _gla(q, k, v, log_a, r, gain, batch):
    t = q.shape[0]
    seq = t // batch
    ts = min(GLA_TS, seq)
    ns = seq // ts
    row = lambda width: pl.BlockSpec((ts, width), lambda b, n: (b * ns + n, 0))
    return pl.pallas_call(
        _gla_kernel,
        out_shape=jax.ShapeDtypeStruct((t, GLA_V_WIDTH), BF16),
        grid=(batch, ns),
        in_specs=[row(GLA_K_WIDTH), row(GLA_K_WIDTH), row(GLA_V_WIDTH), row(GLA_K_WIDTH),
                  row(GLA_V_WIDTH), pl.BlockSpec((1, GLA_DV), lambda b, n: (0, 0))],
        out_specs=row(GLA_V_WIDTH),
        scratch_shapes=[pltpu.VMEM((GLA_HEADS, GLA_DV, GLA_DK), F32)],
        compiler_params=_params("parallel", "arbitrary"),
        name="gla",
    )(q, k, v, log_a, r, gain)


def _outproj_kernel(x_ref, a_ref, g_ref, w_ref, gain_ref, o_ref):
    part = o_ref.shape[0] // OUTPROJ_PARTS
    for r in range(OUTPROJ_PARTS):
        rows = slice(r * part, (r + 1) * part)
        h = (jnp.dot(a_ref[rows, :], w_ref[:ATTN_WIDTH, :], preferred_element_type=F32)
             + jnp.dot(g_ref[rows, :], w_ref[ATTN_WIDTH:, :], preferred_element_type=F32))
        o_ref[rows, :] = x_ref[rows, :] + _rms(h, gain_ref[...])


def _outproj(x, attn, gla, w_out, gain, layer):
    t, d = x.shape
    tm = PROJ_TM
    row = lambda width: pl.BlockSpec((tm, width), lambda i: (i, 0))
    return pl.pallas_call(
        _outproj_kernel,
        out_shape=jax.ShapeDtypeStruct((t, d), F32),
        grid=(t // tm,),
        in_specs=[row(d), row(ATTN_WIDTH), row(GLA_V_WIDTH), _resident(w_out.shape[1:], layer),
                  _resident((1, d))],
        out_specs=row(d),
        compiler_params=_params("parallel"),
        name="mixer_outproj",
    )(x, attn, gla, w_out, gain)


def _prep_ffn_weights(ffn_w_in, ffn_w_out):
    full = (ffn_w_out.shape[-2] // FFN_TF) * FFN_TF
    wgu, wgut = _cast_w_in(ffn_w_in)
    return wgu, _cast_w_out(ffn_w_out), wgut, ffn_w_out[..., full:, :].astype(BF16)


def _prep_mixer_weights(w_mix_in, gla_gate_w2, w_mix_out):
    return w_mix_in.astype(BF16), gla_gate_w2.astype(BF16), w_mix_out.astype(BF16)


def kernel(x, positions, norm_gains, ffn_w_in, ffn_w_out, w_mix_in, attn_sinks, gla_gate_w2,
           gla_gate_b, gla_norm_gain, w_mix_out):
    batch, seq, d = x.shape
    depth = norm_gains.shape[0]
    xt = x.reshape(batch * seq, d)
    cos, sin = _rope_tables(positions)
    w_ffn = _prep_ffn_weights(ffn_w_in, ffn_w_out)
    w_mix, w2, w_mo = _prep_mixer_weights(w_mix_in, gla_gate_w2, w_mix_out)
    for l in range(depth):
        g = norm_gains[l].reshape(-1, 1, d)
        xt = _ffn(xt, g[0], g[1], w_ffn, l, 0)
        qa, ka, va, qg, kg, vg, rg, la = _inproj(
            xt, g[2], w_mix, w2, gla_gate_b[l].reshape(1, -1), cos, sin, l)
        attn = _attention(qa, ka, va, attn_sinks[l], batch)
        gla = _gla(qg, kg, vg, la, rg, gla_norm_gain[l].reshape(1, -1), batch)
        xt = _outproj(xt, attn, gla, w_mo, g[3], l)
        xt = _ffn(xt, g[4], g[5], w_ffn, l, 1)
    return xt.reshape(batch, seq, d)
```

```python
import jax
import jax.numpy as jnp
from jax import lax
from jax.experimental import pallas as pl
from jax.experimental.pallas import tpu as pltpu

F32 = jnp.float32
BF16 = jnp.bfloat16

HEAD_DIM = 64
N_Q_HEADS = 16
N_KV_HEADS = 4
Q_PER_KV = N_Q_HEADS // N_KV_HEADS
ATTN_WIDTH = N_Q_HEADS * HEAD_DIM
KV_WIDTH = N_KV_HEADS * HEAD_DIM
ATTN_BLOCK = 128
ROPE_THETA = 10000.0
GLA_HEADS = 4
GLA_DK = 128
GLA_DV = 256
GLA_K_WIDTH = GLA_HEADS * GLA_DK
GLA_V_WIDTH = GLA_HEADS * GLA_DV
GLA_GATE_RANK = 16
GLA_TAU = 16.0
GLA_CHUNK = 64
NORM_EPS = 1e-6

LANES = 128
VMEM_LIMIT_BYTES = 60000 * 1024

FFN_TM = 1024
FFN_TF = 512
FFN_LAST_PARTS = 4
PROJ_TM = 512
OUTPROJ_PARTS = 1
ATTN_TQ = 1024
GLA_TS = 1024
CAST_ROWS = 256
CAST_OUT_STEPS = 4


def _params(*sem):
    return pltpu.CompilerParams(dimension_semantics=sem, vmem_limit_bytes=VMEM_LIMIT_BYTES)


def _resident(shape, lead=()):
    lead = tuple(lead) if isinstance(lead, (tuple, list)) else (lead,)
    return pl.BlockSpec((None,) * len(lead) + tuple(shape), lambda *_: lead + (0,) * len(shape),
                        pipeline_mode=pl.Buffered(1))


def _rms(x, gain, halve=False):
    ms = jnp.mean(x * x, axis=-1, keepdims=True)
    inv = lax.rsqrt(ms + NORM_EPS)
    if halve:
        inv = inv * 0.5
    return (x * inv) * gain


def _silu(x):
    return x * jax.nn.sigmoid(x)


def _swiglu_chunk(hn, wgu, wo):
    width = wo.shape[0]
    gu = jnp.dot(hn, wgu, preferred_element_type=F32)
    act = (_silu(gu[:, :width]) * gu[:, width:]).astype(BF16)
    return jnp.dot(act, wo, preferred_element_type=F32)


def _ffn_kernel(x_ref, gin_ref, gout_ref, wgu_ref, wo_ref, wgut_ref, wot_ref, o_ref, hn_ref):
    j = pl.program_id(1)
    last = pl.num_programs(1) - 1

    @pl.when(j == 0)
    def _():
        hn = _rms(x_ref[...], gin_ref[...]).astype(BF16)
        hn_ref[...] = hn
        o_ref[...] = _swiglu_chunk(hn, wgu_ref[...], wo_ref[...])

    @pl.when((j > 0) & (j < last))
    def _():
        o_ref[...] += _swiglu_chunk(hn_ref[...], wgu_ref[...], wo_ref[...])

    @pl.when(j == last)
    def _():
        part = o_ref.shape[0] // FFN_LAST_PARTS
        for r in range(FFN_LAST_PARTS):
            rows = slice(r * part, (r + 1) * part)
            h = o_ref[rows, :] + _swiglu_chunk(hn_ref[rows, :], wgut_ref[...], wot_ref[...])
            o_ref[rows, :] = x_ref[rows, :] + _rms(h, gout_ref[...], halve=True)


def _ffn(x, gin, gout, w, layer, slot):
    wgu, wo, wgut, wot = w
    t, d = x.shape
    n_full = wgu.shape[2]
    tail = wot.shape[2]
    assert n_full >= 2 and wo.shape[2] == n_full * FFN_TF + tail and 0 < tail and tail % LANES == 0
    main = lambda j: jnp.minimum(j, n_full - 1)
    return pl.pallas_call(
        _ffn_kernel,
        out_shape=jax.ShapeDtypeStruct((t, d), F32),
        grid=(t // FFN_TM, n_full + 1),
        in_specs=[
            pl.BlockSpec((FFN_TM, d), lambda i, j: (i, 0)),
            pl.BlockSpec((1, d), lambda i, j: (0, 0)),
            pl.BlockSpec((1, d), lambda i, j: (0, 0)),
            pl.BlockSpec((None, None, None, d, 2 * FFN_TF),
                         lambda i, j: (layer, slot, main(j), 0, 0)),
            pl.BlockSpec((None, None, FFN_TF, d), lambda i, j: (layer, slot, main(j), 0)),
            _resident((d, 2 * tail), (layer, slot)),
            _resident((tail, d), (layer, slot)),
        ],
        out_specs=pl.BlockSpec((FFN_TM, d), lambda i, j: (i, 0)),
        scratch_shapes=[pltpu.VMEM((FFN_TM, d), BF16)],
        compiler_params=_params("parallel", "arbitrary"),
        name="ffn",
    )(x, gin, gout, wgu, wo, wgut, wot)


def _cast_w_in_kernel(w_ref, main_ref, tail_ref):
    d_ff = w_ref.shape[1] // 2
    for c in range(main_ref.shape[0]):
        for half in range(2):
            src = half * d_ff + c * FFN_TF
            main_ref[c, :, half * FFN_TF:(half + 1) * FFN_TF] = w_ref[:, src:src + FFN_TF].astype(BF16)
    tail = tail_ref.shape[1] // 2
    for half in range(2):
        src = (half + 1) * d_ff - tail
        tail_ref[:, half * tail:(half + 1) * tail] = w_ref[:, src:src + tail].astype(BF16)


def _cast_w_in(ffn_w_in):
    depth, two, d, f2 = ffn_w_in.shape
    n_full = (f2 // 2) // FFN_TF
    tail = f2 // 2 - n_full * FFN_TF
    rt = CAST_ROWS
    pick = lambda m, i: (m // two, m % two)
    return pl.pallas_call(
        _cast_w_in_kernel,
        out_shape=[jax.ShapeDtypeStruct((depth, two, n_full, d, 2 * FFN_TF), BF16),
                   jax.ShapeDtypeStruct((depth, two, d, 2 * tail), BF16)],
        grid=(depth * two, d // rt),
        in_specs=[pl.BlockSpec((None, None, rt, f2), lambda m, i: pick(m, i) + (i, 0))],
        out_specs=[pl.BlockSpec((None, None, n_full, rt, 2 * FFN_TF), lambda m, i: pick(m, i) + (0, i, 0)),
                   pl.BlockSpec((None, None, rt, 2 * tail), lambda m, i: pick(m, i) + (i, 0))],
        compiler_params=_params("parallel", "parallel"),
        name="cast_ffn_w_in",
    )(ffn_w_in)


def _cast_rows_kernel(w_ref, o_ref):
    o_ref[...] = w_ref[...].astype(BF16)


def _cast_w_out(ffn_w_out):
    depth, two, f, d = ffn_w_out.shape
    rt = f // CAST_OUT_STEPS
    assert rt * CAST_OUT_STEPS == f and rt % 16 == 0
    spec = pl.BlockSpec((None, None, rt, d), lambda m, i: (m // two, m % two, i, 0))
    return pl.pallas_call(
        _cast_rows_kernel,
        out_shape=jax.ShapeDtypeStruct(ffn_w_out.shape, BF16),
        grid=(depth * two, CAST_OUT_STEPS),
        in_specs=[spec],
        out_specs=spec,
        compiler_params=_params("parallel", "parallel"),
        name="cast_ffn_w_out",
    )(ffn_w_out)


def _rope_table_kernel(pos_ref, invf_ref, cos_ref, sin_ref):
    ang = pos_ref[...].astype(F32) * invf_ref[...]
    lane = lax.broadcasted_iota(jnp.int32, ang.shape, 1)
    first_half = (lane % HEAD_DIM) < (HEAD_DIM // 2)
    cos_ref[...] = jnp.cos(ang)
    sin_ref[...] = jnp.where(first_half, -jnp.sin(ang), jnp.sin(ang))


def _rope_tables(positions):
    t = positions.size
    tm = min(t, 2048)
    inv_freq = ROPE_THETA ** (-jnp.arange(0, HEAD_DIM, 2, dtype=F32) / HEAD_DIM)
    invf = jnp.tile(inv_freq, LANES // (HEAD_DIM // 2)).reshape(1, LANES)
    return pl.pallas_call(
        _rope_table_kernel,
        out_shape=[jax.ShapeDtypeStruct((t, LANES), F32)] * 2,
        grid=(t // tm,),
        in_specs=[pl.BlockSpec((tm, 1), lambda i: (i, 0)),
                  pl.BlockSpec((1, LANES), lambda i: (0, 0))],
        out_specs=[pl.BlockSpec((tm, LANES), lambda i: (i, 0))] * 2,
        compiler_params=_params("parallel"),
        name="rope_tables",
    )(positions.reshape(t, 1), invf)


def _rope(x, cos, sin, first_half):
    half = HEAD_DIM // 2
    partner = jnp.where(first_half, pltpu.roll(x, LANES - half, axis=1), pltpu.roll(x, half, axis=1))
    return x * cos + partner * sin


def _inproj_kernel(x_ref, g_ref, w_ref, wlr_ref, w2_ref, gb_ref, cos_ref, sin_ref,
                   qa_ref, ka_ref, va_ref, qg_ref, kg_ref, vg_ref, rg_ref, la_ref):
    hn = _rms(x_ref[...], g_ref[...]).astype(BF16)
    cos = cos_ref[...]
    sin = sin_ref[...]
    lane = lax.broadcasted_iota(jnp.int32, cos.shape, 1)
    first_half = (lane % HEAD_DIM) < (HEAD_DIM // 2)

    def proj(c0, width):
        return jnp.dot(hn, w_ref[:, c0:c0 + width], preferred_element_type=F32)

    def dup_into(dst_ref, s, slab):
        lo = lane < HEAD_DIM
        swapped = pltpu.roll(slab, HEAD_DIM, axis=1)
        dst_ref[:, (2 * s) * LANES:(2 * s + 1) * LANES] = jnp.where(lo, slab, swapped).astype(BF16)
        dst_ref[:, (2 * s + 1) * LANES:(2 * s + 2) * LANES] = jnp.where(lo, swapped, slab).astype(BF16)

    c_gate = ATTN_WIDTH + 2 * KV_WIDTH + 2 * GLA_K_WIDTH + 2 * GLA_V_WIDTH
    g_lr = jnp.dot(hn, wlr_ref[...], preferred_element_type=F32).astype(BF16)
    z = jnp.dot(g_lr, w2_ref[...], preferred_element_type=F32) + gb_ref[...]
    la_ref[...] = jax.nn.log_sigmoid(z) / GLA_TAU

    c = 0
    qa = proj(c, ATTN_WIDTH)
    for s in range(ATTN_WIDTH // LANES):
        cols = slice(s * LANES, (s + 1) * LANES)
        qa_ref[:, cols] = (_rope(qa[:, cols], cos, sin, first_half) * HEAD_DIM ** -0.5).astype(BF16)
    c += ATTN_WIDTH
    kv = proj(c, 2 * KV_WIDTH)
    for s in range(KV_WIDTH // LANES):
        dup_into(ka_ref, s, _rope(kv[:, s * LANES:(s + 1) * LANES], cos, sin, first_half))
        dup_into(va_ref, s, kv[:, KV_WIDTH + s * LANES:KV_WIDTH + (s + 1) * LANES])
    c += 2 * KV_WIDTH
    qg_ref[...] = proj(c, GLA_K_WIDTH)
    c += GLA_K_WIDTH
    kg_ref[...] = proj(c, GLA_K_WIDTH)
    c += GLA_K_WIDTH
    vg_ref[...] = proj(c, GLA_V_WIDTH).astype(BF16)
    c += GLA_V_WIDTH
    rg_ref[...] = proj(c, GLA_V_WIDTH).astype(BF16)
    assert c + GLA_V_WIDTH == c_gate


def _inproj(x, gain, w_mix, w_lr, w2, gate_b, cos, sin, layer):
    t, d = x.shape
    tm = PROJ_TM
    row = lambda width: pl.BlockSpec((tm, width), lambda i: (i, 0))
    widths = (ATTN_WIDTH, 2 * KV_WIDTH, 2 * KV_WIDTH, GLA_K_WIDTH, GLA_K_WIDTH, GLA_V_WIDTH,
              GLA_V_WIDTH, GLA_K_WIDTH)
    dtypes = (BF16, BF16, BF16, F32, F32, BF16, BF16, F32)
    return pl.pallas_call(
        _inproj_kernel,
        out_shape=[jax.ShapeDtypeStruct((t, w), dt) for w, dt in zip(widths, dtypes)],
        grid=(t // tm,),
        in_specs=[row(d), _resident((1, d)), _resident(w_mix.shape[1:], layer),
                  _resident(w_lr.shape[1:], layer), _resident(w2.shape[1:], layer),
                  _resident((1, GLA_K_WIDTH)), row(LANES), row(LANES)],
        out_specs=[row(w) for w in widths],
        compiler_params=_params("parallel"),
        name="mixer_inproj",
    )(x, gain, w_mix, w_lr, w2, gate_b, cos, sin)


def _attn_kernel(sinks_ref, q_ref, kc_ref, vc_ref, kp_ref, vp_ref, o_ref):
    n = pl.program_id(1)
    blk = ATTN_BLOCK
    k_all = jnp.concatenate([kp_ref[...], kc_ref[...]], axis=0)
    v_all = jnp.concatenate([vp_ref[...], vc_ref[...]], axis=0)
    qi = lax.broadcasted_iota(jnp.int32, (2 * blk, blk), 0) % blk
    kj = lax.broadcasted_iota(jnp.int32, (2 * blk, blk), 1)
    top = lax.broadcasted_iota(jnp.int32, (2 * blk, 1), 0) < blk
    band_prev = kj > qi
    band_cur = kj <= qi
    band_prev_first = band_prev & (kj >= jnp.where(n > 0, 0, blk))
    lane = lax.broadcasted_iota(jnp.int32, (2 * blk, LANES), 1)
    key = lax.broadcasted_iota(jnp.int32, (2 * blk, LANES), 0)
    half = (lane < HEAD_DIM, lane >= HEAD_DIM)
    ones_half = tuple(jnp.where(hm, 1.0, 0.0).astype(BF16) for hm in half)
    zero = jnp.zeros((2 * blk, LANES), BF16)
    neg_inf = jnp.full((2 * blk, blk), -jnp.inf, F32)
    for g in range(N_KV_HEADS):
        gl = slice(g * LANES, (g + 1) * LANES)
        fills = []
        for parity in range(2):
            sink = jnp.where(top, sinks_ref[Q_PER_KV * g + parity], sinks_ref[Q_PER_KV * g + 2 + parity])
            fills.append(jnp.where(kj == 0, sink, neg_inf))
        for b in range(q_ref.shape[0] // blk):
            rows = slice(b * blk, (b + 1) * blk)
            q = jnp.concatenate([q_ref[rows, (2 * g) * LANES:(2 * g + 1) * LANES],
                                 q_ref[rows, (2 * g + 1) * LANES:(2 * g + 2) * LANES]], axis=0)
            kd = k_all[b * blk:(b + 2) * blk, gl]
            vd = v_all[b * blk:(b + 2) * blk, gl]
            mask_prev = band_prev_first if b == 0 else band_prev
            acc = None
            for parity in range(2):
                kz = jnp.where(half[parity], kd, zero)
                s = lax.dot_general(q, kz, (((1,), (1,)), ((), ())), preferred_element_type=F32)
                s_prev = jnp.where(mask_prev, s[:, :blk], fills[parity])
                s_cur = jnp.where(band_cur, s[:, blk:], neg_inf)
                m = jnp.max(jnp.maximum(s_prev, s_cur), axis=-1, keepdims=True)
                p = jnp.concatenate([jnp.exp(s_prev - m), jnp.exp(s_cur - m)], axis=1).astype(BF16)
                vz = jnp.where(half[parity] & (key > 0), vd, zero)
                w = jnp.concatenate([vz, ones_half[parity]], axis=1)
                part = jnp.dot(p, w, preferred_element_type=F32)
                acc = part if acc is None else acc + part
            out = (acc[:, :LANES] / acc[:, LANES:]).astype(BF16)
            o_ref[rows, (2 * g) * LANES:(2 * g + 1) * LANES] = out[:blk]
            o_ref[rows, (2 * g + 1) * LANES:(2 * g + 2) * LANES] = out[blk:]


def _attention(q, k, v, sinks, batch):
    t = q.shape[0]
    seq = t // batch
    tq = min(ATTN_TQ, seq)
    nq = seq // tq
    per = tq // ATTN_BLOCK
    kvw = k.shape[1]
    cur = lambda width: pl.BlockSpec((tq, width), lambda b, n: (b * nq + n, 0))
    prev = lambda width: pl.BlockSpec(
        (ATTN_BLOCK, width), lambda b, n: (jnp.maximum((b * nq + n) * per - 1, 0), 0))
    return pl.pallas_call(
        _attn_kernel,
        out_shape=jax.ShapeDtypeStruct((t, ATTN_WIDTH), BF16),
        grid=(batch, nq),
        in_specs=[pl.BlockSpec(memory_space=pltpu.SMEM),
                  cur(ATTN_WIDTH), cur(kvw), cur(kvw), prev(kvw), prev(kvw)],
        out_specs=cur(ATTN_WIDTH),
        compiler_params=_params("parallel", "arbitrary"),
        name="swa_attention",
    )(sinks, q, k, v, k, v)


def _gla_kernel(q_ref, k_ref, v_ref, la_ref, r_ref, gain_ref, o_ref, state_ref):
    @pl.when(pl.program_id(1) == 0)
    def _():
        state_ref[...] = jnp.zeros_like(state_ref)

    c = GLA_CHUNK
    pc = 2 * c
    ti = lax.broadcasted_iota(jnp.int32, (pc, pc), 0)
    si = lax.broadcasted_iota(jnp.int32, (pc, pc), 1)
    same_chunk = (ti >= c) == (si >= c)
    causal = same_chunk & (ti >= si)
    cross = (ti >= c) & (si < c)
    tri = causal.astype(F32)
    second = lax.broadcasted_iota(jnp.int32, (pc, 1), 0) >= c
    gain = gain_ref[...]
    scale = GLA_DK ** -0.5
    nt = (((1,), (1,)), ((), ()))
    state_t = [state_ref[h] for h in range(GLA_HEADS)]
    for pi in range(q_ref.shape[0] // pc):
        rows = slice(pi * pc, (pi + 1) * pc)
        la = la_ref[rows, :]
        b = jnp.dot(tri, la, precision=lax.Precision.HIGHEST, preferred_element_type=F32)
        b_last0 = b[c - 1:c, :]
        b_last1 = b[pc - 1:pc, :]
        decay0 = jnp.exp(b_last0)
        decay1 = jnp.exp(b_last1)
        q_dec = (q_ref[rows, :] * scale) * jnp.exp(b)
        k_all = k_ref[rows, :]
        k_dec = k_all * jnp.exp(-b)
        k_rem = k_all * jnp.exp(jnp.where(second, b_last1, b_last0) - b)
        q_in = jnp.where(second, q_dec * decay0, q_dec)
        k_out = jnp.where(second, k_rem, k_rem * decay1)
        for h in range(GLA_HEADS):
            kl = slice(h * GLA_DK, (h + 1) * GLA_DK)
            vl = slice(h * GLA_DV, (h + 1) * GLA_DV)
            qh = q_dec[:, kl].astype(BF16)
            qih = q_in[:, kl].astype(BF16)
            v_t = v_ref[rows, vl].astype(F32).T.astype(BF16)
            s2 = lax.dot_general(jnp.concatenate([qh, qih], axis=0), k_dec[:, kl].astype(BF16), nt,
                                 preferred_element_type=F32)
            intra = (jnp.where(causal, s2[:pc], 0.0) + jnp.where(cross, s2[pc:], 0.0)).astype(BF16)
            o = lax.dot_general(jnp.concatenate([qih, intra], axis=1),
                                jnp.concatenate([state_t[h].astype(BF16), v_t], axis=1),
                                nt, preferred_element_type=F32)
            d_state_t = jnp.dot(v_t, k_out[:, kl].astype(BF16), preferred_element_type=F32)
            state_t[h] = state_t[h] * (decay0[:, kl] * decay1[:, kl]) + d_state_t
            r = r_ref[rows, vl].astype(F32)
            o_ref[rows, vl] = (_rms(o, gain) * _silu(r)).astype(BF16)
    for h in range(GLA_HEADS):
        state_ref[h] = state_t[h]


def _gla(q, k, v, log_a, r, gain, batch):
    t = q.shape[0]
    seq = t // batch
    ts = min(GLA_TS, seq)
    ns = seq // ts
    row = lambda width: pl.BlockSpec((ts, width), lambda b, n: (b * ns + n, 0))
    return pl.pallas_call(
        _gla_kernel,
        out_shape=jax.ShapeDtypeStruct((t, GLA_V_WIDTH), BF16),
        grid=(batch, ns),
        in_specs=[row(GLA_K_WIDTH), row(GLA_K_WIDTH), row(GLA_V_WIDTH), row(GLA_K_WIDTH),
                  row(GLA_V_WIDTH), pl.BlockSpec((1, GLA_DV), lambda b, n: (0, 0))],
        out_specs=row(GLA_V_WIDTH),
        scratch_shapes=[pltpu.VMEM((GLA_HEADS, GLA_DV, GLA_DK), F32)],
        compiler_params=_params("parallel", "arbitrary"),
        name="gla",
    )(q, k, v, log_a, r, gain)


def _outproj_kernel(x_ref, a_ref, g_ref, w_ref, gain_ref, o_ref):
    part = o_ref.shape[0] // OUTPROJ_PARTS
    for r in range(OUTPROJ_PARTS):
        rows = slice(r * part, (r + 1) * part)
        h = (jnp.dot(a_ref[rows, :], w_ref[:ATTN_WIDTH, :], preferred_element_type=F32)
             + jnp.dot(g_ref[rows, :], w_ref[ATTN_WIDTH:, :], preferred_element_type=F32))
        o_ref[rows, :] = x_ref[rows, :] + _rms(h, gain_ref[...])


def _outproj(x, attn, gla, w_out, gain, layer):
    t, d = x.shape
    tm = PROJ_TM
    row = lambda width: pl.BlockSpec((tm, width), lambda i: (i, 0))
    return pl.pallas_call(
        _outproj_kernel,
        out_shape=jax.ShapeDtypeStruct((t, d), F32),
        grid=(t // tm,),
        in_specs=[row(d), row(ATTN_WIDTH), row(GLA_V_WIDTH), _resident(w_out.shape[1:], layer),
                  _resident((1, d))],
        out_specs=row(d),
        compiler_params=_params("parallel"),
        name="mixer_outproj",
    )(x, attn, gla, w_out, gain)


def _prep_ffn_weights(ffn_w_in, ffn_w_out):
    full = (ffn_w_out.shape[-2] // FFN_TF) * FFN_TF
    wgu, wgut = _cast_w_in(ffn_w_in)
    return wgu, _cast_w_out(ffn_w_out), wgut, ffn_w_out[..., full:, :].astype(BF16)


def _prep_mixer_weights(w_mix_in, gla_gate_w2, w_mix_out):
    main = w_mix_in.shape[-1] - GLA_GATE_RANK
    return (w_mix_in[..., :main].astype(BF16), w_mix_in[..., main:].astype(BF16),
            gla_gate_w2.astype(BF16), w_mix_out.astype(BF16))


def kernel(x, positions, norm_gains, ffn_w_in, ffn_w_out, w_mix_in, attn_sinks, gla_gate_w2,
           gla_gate_b, gla_norm_gain, w_mix_out):
    batch, seq, d = x.shape
    depth = norm_gains.shape[0]
    xt = x.reshape(batch * seq, d)
    cos, sin = _rope_tables(positions)
    w_ffn = _prep_ffn_weights(ffn_w_in, ffn_w_out)
    w_mix, w_lr, w2, w_mo = _prep_mixer_weights(w_mix_in, gla_gate_w2, w_mix_out)
    for l in range(depth):
        g = norm_gains[l].reshape(-1, 1, d)
        xt = _ffn(xt, g[0], g[1], w_ffn, l, 0)
        qa, ka, va, qg, kg, vg, rg, la = _inproj(
            xt, g[2], w_mix, w_lr, w2, gla_gate_b[l].reshape(1, -1), cos, sin, l)
        attn = _attention(qa, ka, va, attn_sinks[l], batch)
        gla = _gla(qg, kg, vg, la, rg, gla_norm_gain[l].reshape(1, -1), batch)
        xt = _outproj(xt, attn, gla, w_mo, g[3], l)
        xt = _ffn(xt, g[4], g[5], w_ffn, l, 1)
    return xt.reshape(batch, seq, d)
```

```python
import jax
import jax.numpy as jnp
from jax import lax
from jax.experimental import pallas as pl
from jax.experimental.pallas import tpu as pltpu

F32 = jnp.float32
BF16 = jnp.bfloat16

HEAD_DIM = 64
N_Q_HEADS = 16
N_KV_HEADS = 4
Q_PER_KV = N_Q_HEADS // N_KV_HEADS
ATTN_WIDTH = N_Q_HEADS * HEAD_DIM
KV_WIDTH = N_KV_HEADS * HEAD_DIM
ATTN_BLOCK = 128
ROPE_THETA = 10000.0
GLA_HEADS = 4
GLA_DK = 128
GLA_DV = 256
GLA_K_WIDTH = GLA_HEADS * GLA_DK
GLA_V_WIDTH = GLA_HEADS * GLA_DV
GLA_GATE_RANK = 16
GLA_TAU = 16.0
GLA_CHUNK = 64
NORM_EPS = 1e-6

LANES = 128
VMEM_LIMIT_BYTES = 60000 * 1024

FFN_TM = 1024
FFN_TF = 512
FFN_LAST_PARTS = 4
PROJ_TM = 512
OUTPROJ_PARTS = 1
ATTN_TQ = 1024
GLA_TS = 1024
CAST_ROWS = 256
CAST_OUT_STEPS = 4


def _params(*sem):
    return pltpu.CompilerParams(dimension_semantics=sem, vmem_limit_bytes=VMEM_LIMIT_BYTES)


def _resident(shape, lead=()):
    lead = tuple(lead) if isinstance(lead, (tuple, list)) else (lead,)
    return pl.BlockSpec((None,) * len(lead) + tuple(shape), lambda *_: lead + (0,) * len(shape),
                        pipeline_mode=pl.Buffered(1))


def _rms(x, gain, halve=False):
    ms = jnp.mean(x * x, axis=-1, keepdims=True)
    inv = lax.rsqrt(ms + NORM_EPS)
    if halve:
        inv = inv * 0.5
    return (x * inv) * gain


def _silu(x):
    return x * jax.nn.sigmoid(x)


def _swiglu_chunk(hn, wgu, wo):
    width = wo.shape[0]
    gu = jnp.dot(hn, wgu, preferred_element_type=F32)
    act = (_silu(gu[:, :width]) * gu[:, width:]).astype(BF16)
    return jnp.dot(act, wo, preferred_element_type=F32)


def _ffn_kernel(x_ref, gin_ref, gout_ref, wgu_ref, wo_ref, wgut_ref, wot_ref, o_ref, hn_ref):
    j = pl.program_id(1)
    last = pl.num_programs(1) - 1

    @pl.when(j == 0)
    def _():
        hn = _rms(x_ref[...], gin_ref[...]).astype(BF16)
        hn_ref[...] = hn
        o_ref[...] = _swiglu_chunk(hn, wgu_ref[...], wo_ref[...])

    @pl.when((j > 0) & (j < last))
    def _():
        o_ref[...] += _swiglu_chunk(hn_ref[...], wgu_ref[...], wo_ref[...])

    @pl.when(j == last)
    def _():
        part = o_ref.shape[0] // FFN_LAST_PARTS
        for r in range(FFN_LAST_PARTS):
            rows = slice(r * part, (r + 1) * part)
            h = o_ref[rows, :] + _swiglu_chunk(hn_ref[rows, :], wgut_ref[...], wot_ref[...])
            o_ref[rows, :] = x_ref[rows, :] + _rms(h, gout_ref[...], halve=True)


def _ffn(x, gin, gout, w, layer, slot):
    wgu, wo, wgut, wot = w
    t, d = x.shape
    n_full = wgu.shape[2]
    tail = wot.shape[2]
    assert n_full >= 2 and wo.shape[2] == n_full * FFN_TF + tail and 0 < tail and tail % LANES == 0
    main = lambda j: jnp.minimum(j, n_full - 1)
    return pl.pallas_call(
        _ffn_kernel,
        out_shape=jax.ShapeDtypeStruct((t, d), F32),
        grid=(t // FFN_TM, n_full + 1),
        in_specs=[
            pl.BlockSpec((FFN_TM, d), lambda i, j: (i, 0)),
            pl.BlockSpec((1, d), lambda i, j: (0, 0)),
            pl.BlockSpec((1, d), lambda i, j: (0, 0)),
            pl.BlockSpec((None, None, None, d, 2 * FFN_TF),
                         lambda i, j: (layer, slot, main(j), 0, 0)),
            pl.BlockSpec((None, None, FFN_TF, d), lambda i, j: (layer, slot, main(j), 0)),
            _resident((d, 2 * tail), (layer, slot)),
            _resident((tail, d), (layer, slot)),
        ],
        out_specs=pl.BlockSpec((FFN_TM, d), lambda i, j: (i, 0)),
        scratch_shapes=[pltpu.VMEM((FFN_TM, d), BF16)],
        compiler_params=_params("parallel", "arbitrary"),
        name="ffn",
    )(x, gin, gout, wgu, wo, wgut, wot)


def _cast_w_in_kernel(w_ref, main_ref, tail_ref):
    d_ff = w_ref.shape[1] // 2
    for c in range(main_ref.shape[0]):
        for half in range(2):
            src = half * d_ff + c * FFN_TF
            main_ref[c, :, half * FFN_TF:(half + 1) * FFN_TF] = w_ref[:, src:src + FFN_TF].astype(BF16)
    tail = tail_ref.shape[1] // 2
    for half in range(2):
        src = (half + 1) * d_ff - tail
        tail_ref[:, half * tail:(half + 1) * tail] = w_ref[:, src:src + tail].astype(BF16)


def _cast_w_in(ffn_w_in):
    depth, two, d, f2 = ffn_w_in.shape
    n_full = (f2 // 2) // FFN_TF
    tail = f2 // 2 - n_full * FFN_TF
    rt = CAST_ROWS
    pick = lambda m, i: (m // two, m % two)
    return pl.pallas_call(
        _cast_w_in_kernel,
        out_shape=[jax.ShapeDtypeStruct((depth, two, n_full, d, 2 * FFN_TF), BF16),
                   jax.ShapeDtypeStruct((depth, two, d, 2 * tail), BF16)],
        grid=(depth * two, d // rt),
        in_specs=[pl.BlockSpec((None, None, rt, f2), lambda m, i: pick(m, i) + (i, 0))],
        out_specs=[pl.BlockSpec((None, None, n_full, rt, 2 * FFN_TF), lambda m, i: pick(m, i) + (0, i, 0)),
                   pl.BlockSpec((None, None, rt, 2 * tail), lambda m, i: pick(m, i) + (i, 0))],
        compiler_params=_params("parallel", "parallel"),
        name="cast_ffn_w_in",
    )(ffn_w_in)


def _cast_rows_kernel(w_ref, o_ref):
    o_ref[...] = w_ref[...].astype(BF16)


def _cast_w_out(ffn_w_out):
    depth, two, f, d = ffn_w_out.shape
    rt = f // CAST_OUT_STEPS
    assert rt * CAST_OUT_STEPS == f and rt % 16 == 0
    spec = pl.BlockSpec((None, None, rt, d), lambda m, i: (m // two, m % two, i, 0))
    return pl.pallas_call(
        _cast_rows_kernel,
        out_shape=jax.ShapeDtypeStruct(ffn_w_out.shape, BF16),
        grid=(depth * two, CAST_OUT_STEPS),
        in_specs=[spec],
        out_specs=spec,
        compiler_params=_params("parallel", "parallel"),
        name="cast_ffn_w_out",
    )(ffn_w_out)


def _rope_table_kernel(pos_ref, invf_ref, cos_ref, sin_ref):
    ang = pos_ref[...].astype(F32) * invf_ref[...]
    lane = lax.broadcasted_iota(jnp.int32, ang.shape, 1)
    first_half = (lane % HEAD_DIM) < (HEAD_DIM // 2)
    cos_ref[...] = jnp.cos(ang)
    sin_ref[...] = jnp.where(first_half, -jnp.sin(ang), jnp.sin(ang))


def _rope_tables(positions):
    t = positions.size
    tm = min(t, 2048)
    inv_freq = ROPE_THETA ** (-jnp.arange(0, HEAD_DIM, 2, dtype=F32) / HEAD_DIM)
    invf = jnp.tile(inv_freq, LANES // (HEAD_DIM // 2)).reshape(1, LANES)
    return pl.pallas_call(
        _rope_table_kernel,
        out_shape=[jax.ShapeDtypeStruct((t, LANES), F32)] * 2,
        grid=(t // tm,),
        in_specs=[pl.BlockSpec((tm, 1), lambda i: (i, 0)),
                  pl.BlockSpec((1, LANES), lambda i: (0, 0))],
        out_specs=[pl.BlockSpec((tm, LANES), lambda i: (i, 0))] * 2,
        compiler_params=_params("parallel"),
        name="rope_tables",
    )(positions.reshape(t, 1), invf)


def _rope(x, cos, sin, first_half):
    half = HEAD_DIM // 2
    partner = jnp.where(first_half, pltpu.roll(x, LANES - half, axis=1), pltpu.roll(x, half, axis=1))
    return x * cos + partner * sin


def _inproj_kernel(x_ref, g_ref, w_ref, w2_ref, gb_ref, cos_ref, sin_ref,
                   qa_ref, ka_ref, va_ref, qg_ref, kg_ref, vg_ref, rg_ref, la_ref):
    hn = _rms(x_ref[...], g_ref[...]).astype(BF16)
    cos = cos_ref[...]
    sin = sin_ref[...]
    lane = lax.broadcasted_iota(jnp.int32, cos.shape, 1)
    first_half = (lane % HEAD_DIM) < (HEAD_DIM // 2)

    def proj(c0, width):
        return jnp.dot(hn, w_ref[:, c0:c0 + width], preferred_element_type=F32)

    def dup_into(dst_ref, s, slab):
        lo = lane < HEAD_DIM
        swapped = pltpu.roll(slab, HEAD_DIM, axis=1)
        dst_ref[:, (2 * s) * LANES:(2 * s + 1) * LANES] = jnp.where(lo, slab, swapped).astype(BF16)
        dst_ref[:, (2 * s + 1) * LANES:(2 * s + 2) * LANES] = jnp.where(lo, swapped, slab).astype(BF16)

    c_gate = ATTN_WIDTH + 2 * KV_WIDTH + 2 * GLA_K_WIDTH + 2 * GLA_V_WIDTH
    g_lr = proj(c_gate, w_ref.shape[1] - c_gate).astype(BF16)
    z = jnp.dot(g_lr, w2_ref[...], preferred_element_type=F32) + gb_ref[...]
    la_ref[...] = jax.nn.log_sigmoid(z) / GLA_TAU

    c = 0
    qa = proj(c, ATTN_WIDTH)
    for s in range(ATTN_WIDTH // LANES):
        cols = slice(s * LANES, (s + 1) * LANES)
        qa_ref[:, cols] = (_rope(qa[:, cols], cos, sin, first_half) * HEAD_DIM ** -0.5).astype(BF16)
    c += ATTN_WIDTH
    kv = proj(c, 2 * KV_WIDTH)
    for s in range(KV_WIDTH // LANES):
        dup_into(ka_ref, s, _rope(kv[:, s * LANES:(s + 1) * LANES], cos, sin, first_half))
        dup_into(va_ref, s, kv[:, KV_WIDTH + s * LANES:KV_WIDTH + (s + 1) * LANES])
    c += 2 * KV_WIDTH
    qg_ref[...] = proj(c, GLA_K_WIDTH)
    c += GLA_K_WIDTH
    kg_ref[...] = proj(c, GLA_K_WIDTH)
    c += GLA_K_WIDTH
    vg_ref[...] = proj(c, GLA_V_WIDTH).astype(BF16)
    c += GLA_V_WIDTH
    rg_ref[...] = proj(c, GLA_V_WIDTH).astype(BF16)
    assert c + GLA_V_WIDTH == c_gate


def _inproj(x, gain, w_mix, w2, gate_b, cos, sin, layer):
    t, d = x.shape
    tm = PROJ_TM
    row = lambda width: pl.BlockSpec((tm, width), lambda i: (i, 0))
    widths = (ATTN_WIDTH, 2 * KV_WIDTH, 2 * KV_WIDTH, GLA_K_WIDTH, GLA_K_WIDTH, GLA_V_WIDTH,
              GLA_V_WIDTH, GLA_K_WIDTH)
    dtypes = (BF16, BF16, BF16, F32, F32, BF16, BF16, F32)
    return pl.pallas_call(
        _inproj_kernel,
        out_shape=[jax.ShapeDtypeStruct((t, w), dt) for w, dt in zip(widths, dtypes)],
        grid=(t // tm,),
        in_specs=[row(d), _resident((1, d)), _resident(w_mix.shape[1:], layer),
                  _resident(w2.shape[1:], layer), _resident((1, GLA_K_WIDTH)),
                  row(LANES), row(LANES)],
        out_specs=[row(w) for w in widths],
        compiler_params=_params("parallel"),
        name="mixer_inproj",
    )(x, gain, w_mix, w2, gate_b, cos, sin)


def _attn_kernel(sinks_ref, q_ref, kc_ref, vc_ref, kp_ref, vp_ref, o_ref):
    n = pl.program_id(1)
    blk = ATTN_BLOCK
    k_all = jnp.concatenate([kp_ref[...], kc_ref[...]], axis=0)
    v_all = jnp.concatenate([vp_ref[...], vc_ref[...]], axis=0)
    qi = lax.broadcasted_iota(jnp.int32, (2 * blk, blk), 0) % blk
    kj = lax.broadcasted_iota(jnp.int32, (2 * blk, blk), 1)
    top = lax.broadcasted_iota(jnp.int32, (2 * blk, 1), 0) < blk
    band_prev = kj > qi
    band_cur = kj <= qi
    band_prev_first = band_prev & (kj >= jnp.where(n > 0, 0, blk))
    lane = lax.broadcasted_iota(jnp.int32, (2 * blk, LANES), 1)
    key = lax.broadcasted_iota(jnp.int32, (2 * blk, LANES), 0)
    half = (lane < HEAD_DIM, lane >= HEAD_DIM)
    ones_half = tuple(jnp.where(hm, 1.0, 0.0).astype(BF16) for hm in half)
    zero = jnp.zeros((2 * blk, LANES), BF16)
    neg_inf = jnp.full((2 * blk, blk), -jnp.inf, F32)
    for g in range(N_KV_HEADS):
        gl = slice(g * LANES, (g + 1) * LANES)
        fills = []
        for parity in range(2):
            sink = jnp.where(top, sinks_ref[Q_PER_KV * g + parity], sinks_ref[Q_PER_KV * g + 2 + parity])
            fills.append(jnp.where(kj == 0, sink, neg_inf))
        for b in range(q_ref.shape[0] // blk):
            rows = slice(b * blk, (b + 1) * blk)
            q = jnp.concatenate([q_ref[rows, (2 * g) * LANES:(2 * g + 1) * LANES],
                                 q_ref[rows, (2 * g + 1) * LANES:(2 * g + 2) * LANES]], axis=0)
            kd = k_all[b * blk:(b + 2) * blk, gl]
            vd = v_all[b * blk:(b + 2) * blk, gl]
            mask_prev = band_prev_first if b == 0 else band_prev
            acc = None
            for parity in range(2):
                kz = jnp.where(half[parity], kd, zero)
                s = lax.dot_general(q, kz, (((1,), (1,)), ((), ())), preferred_element_type=F32)
                s_prev = jnp.where(mask_prev, s[:, :blk], fills[parity])
                s_cur = jnp.where(band_cur, s[:, blk:], neg_inf)
                m = jnp.max(jnp.maximum(s_prev, s_cur), axis=-1, keepdims=True)
                p = jnp.concatenate([jnp.exp(s_prev - m), jnp.exp(s_cur - m)], axis=1).astype(BF16)
                vz = jnp.where(half[parity] & (key > 0), vd, zero)
                w = jnp.concatenate([vz, ones_half[parity]], axis=1)
                part = jnp.dot(p, w, preferred_element_type=F32)
                acc = part if acc is None else acc + part
            out = (acc[:, :LANES] / acc[:, LANES:]).astype(BF16)
            o_ref[rows, (2 * g) * LANES:(2 * g + 1) * LANES] = out[:blk]
            o_ref[rows, (2 * g + 1) * LANES:(2 * g + 2) * LANES] = out[blk:]


def _attention(q, k, v, sinks, batch):
    t = q.shape[0]
    seq = t // batch
    tq = min(ATTN_TQ, seq)
    nq = seq // tq
    per = tq // ATTN_BLOCK
    kvw = k.shape[1]
    cur = lambda width: pl.BlockSpec((tq, width), lambda b, n: (b * nq + n, 0))
    prev = lambda width: pl.BlockSpec(
        (ATTN_BLOCK, width), lambda b, n: (jnp.maximum((b * nq + n) * per - 1, 0), 0))
    return pl.pallas_call(
        _attn_kernel,
        out_shape=jax.ShapeDtypeStruct((t, ATTN_WIDTH), BF16),
        grid=(batch, nq),
        in_specs=[pl.BlockSpec(memory_space=pltpu.SMEM),
                  cur(ATTN_WIDTH), cur(kvw), cur(kvw), prev(kvw), prev(kvw)],
        out_specs=cur(ATTN_WIDTH),
        compiler_params=_params("parallel", "arbitrary"),
        name="swa_attention",
    )(sinks, q, k, v, k, v)


def _gla_kernel(q_ref, k_ref, v_ref, la_ref, r_ref, gain_ref, o_ref, state_ref):
    @pl.when(pl.program_id(1) == 0)
    def _():
        state_ref[...] = jnp.zeros_like(state_ref)

    c = GLA_CHUNK
    pc = 2 * c
    ti = lax.broadcasted_iota(jnp.int32, (pc, pc), 0)
    si = lax.broadcasted_iota(jnp.int32, (pc, pc), 1)
    same_chunk = (ti >= c) == (si >= c)
    causal = same_chunk & (ti >= si)
    cross = (ti >= c) & (si < c)
    tri = causal.astype(F32)
    second = lax.broadcasted_iota(jnp.int32, (pc, 1), 0) >= c
    gain = gain_ref[...]
    scale = GLA_DK ** -0.5
    nt = (((1,), (1,)), ((), ()))
    state_t = [state_ref[h] for h in range(GLA_HEADS)]
    for pi in range(q_ref.shape[0] // pc):
        rows = slice(pi * pc, (pi + 1) * pc)
        la = la_ref[rows, :]
        b = jnp.dot(tri, la, precision=lax.Precision.HIGHEST, preferred_element_type=F32)
        b_last0 = b[c - 1:c, :]
        b_last1 = b[pc - 1:pc, :]
        decay0 = jnp.exp(b_last0)
        decay1 = jnp.exp(b_last1)
        q_dec = (q_ref[rows, :] * scale) * jnp.exp(b)
        k_all = k_ref[rows, :]
        k_dec = k_all * jnp.exp(-b)
        k_rem = k_all * jnp.exp(jnp.where(second, b_last1, b_last0) - b)
        q_in = jnp.where(second, q_dec * decay0, q_dec)
        k_out = jnp.where(second, k_rem, k_rem * decay1)
        for h in range(GLA_HEADS):
            kl = slice(h * GLA_DK, (h + 1) * GLA_DK)
            vl = slice(h * GLA_DV, (h + 1) * GLA_DV)
            qh = q_dec[:, kl].astype(BF16)
            qih = q_in[:, kl].astype(BF16)
            v_t = v_ref[rows, vl].astype(F32).T.astype(BF16)
            s2 = lax.dot_general(jnp.concatenate([qh, qih], axis=0), k_dec[:, kl].astype(BF16), nt,
                                 preferred_element_type=F32)
            intra = (jnp.where(causal, s2[:pc], 0.0) + jnp.where(cross, s2[pc:], 0.0)).astype(BF16)
            o = lax.dot_general(jnp.concatenate([qih, intra], axis=1),
                                jnp.concatenate([state_t[h].astype(BF16), v_t], axis=1),
                                nt, preferred_element_type=F32)
            d_state_t = jnp.dot(v_t, k_out[:, kl].astype(BF16), preferred_element_type=F32)
            state_t[h] = state_t[h] * (decay0[:, kl] * decay1[:, kl]) + d_state_t
            r = r_ref[rows, vl].astype(F32)
            o_ref[rows, vl] = (_rms(o, gain) * _silu(r)).astype(BF16)
    for h in range(GLA_HEADS):
        state_ref[h] = state_t[h]


def _gla(q, k, v, log_a, r, gain, batch):
    t = q.shape[0]
    seq = t // batch
    ts = min(GLA_TS, seq)
    ns = seq // ts
    row = lambda width: pl.BlockSpec((ts, width), lambda b, n: (b * ns + n, 0))
    return pl.pallas_call(
        _gla_kernel,
        out_shape=jax.ShapeDtypeStruct((t, GLA_V_WIDTH), BF16),
        grid=(batch, ns),
        in_specs=[row(GLA_K_WIDTH), row(GLA_K_WIDTH), row(GLA_V_WIDTH), row(GLA_K_WIDTH),
                  row(GLA_V_WIDTH), pl.BlockSpec((1, GLA_DV), lambda b, n: (0, 0))],
        out_specs=row(GLA_V_WIDTH),
        scratch_shapes=[pltpu.VMEM((GLA_HEADS, GLA_DV, GLA_DK), F32)],
        compiler_params=_params("parallel", "arbitrary"),
        name="gla",
    )(q, k, v, log_a, r, gain)


def _outproj_kernel(x_ref, a_ref, g_ref, w_ref, gain_ref, o_ref):
    part = o_ref.shape[0] // OUTPROJ_PARTS
    for r in range(OUTPROJ_PARTS):
        rows = slice(r * part, (r + 1) * part)
        h = (jnp.dot(a_ref[rows, :], w_ref[:ATTN_WIDTH, :], preferred_element_type=F32)
             + jnp.dot(g_ref[rows, :], w_ref[ATTN_WIDTH:, :], preferred_element_type=F32))
        o_ref[rows, :] = x_ref[rows, :] + _rms(h, gain_ref[...])


def _outproj(x, attn, gla, w_out, gain, layer):
    t, d = x.shape
    tm = PROJ_TM
    row = lambda width: pl.BlockSpec((tm, width), lambda i: (i, 0))
    return pl.pallas_call(
        _outproj_kernel,
        out_shape=jax.ShapeDtypeStruct((t, d), F32),
        grid=(t // tm,),
        in_specs=[row(d), row(ATTN_WIDTH), row(GLA_V_WIDTH), _resident(w_out.shape[1:], layer),
                  _resident((1, d))],
        out_specs=row(d),
        compiler_params=_params("parallel"),
        name="mixer_outproj",
    )(x, attn, gla, w_out, gain)


def _prep_ffn_weights(ffn_w_in, ffn_w_out):
    full = (ffn_w_out.shape[-2] // FFN_TF) * FFN_TF
    wgu, wgut = _cast_w_in(ffn_w_in)
    return wgu, _cast_w_out(ffn_w_out), wgut, ffn_w_out[..., full:, :].astype(BF16)


def _prep_mixer_weights(w_mix_in, gla_gate_w2, w_mix_out):
    return w_mix_in.astype(BF16), gla_gate_w2.astype(BF16), w_mix_out.astype(BF16)


def kernel(x, positions, norm_gains, ffn_w_in, ffn_w_out, w_mix_in, attn_sinks, gla_gate_w2,
           gla_gate_b, gla_norm_gain, w_mix_out):
    batch, seq, d = x.shape
    depth = norm_gains.shape[0]
    xt = x.reshape(batch * seq, d)
    cos, sin = _rope_tables(positions)
    w_ffn = _prep_ffn_weights(ffn_w_in, ffn_w_out)
    w_mix, w2, w_mo = _prep_mixer_weights(w_mix_in, gla_gate_w2, w_mix_out)
    for l in range(depth):
        g = norm_gains[l].reshape(-1, 1, d)
        xt = _ffn(xt, g[0], g[1], w_ffn, l, 0)
        qa, ka, va, qg, kg, vg, rg, la = _inproj(
            xt, g[2], w_mix, w2, gla_gate_b[l].reshape(1, -1), cos, sin, l)
        attn = _attention(qa, ka, va, attn_sinks[l], batch)
        gla = _gla(qg, kg, vg, la, rg, gla_norm_gain[l].reshape(1, -1), batch)
        xt = _outproj(xt, attn, gla, w_mo, g[3], l)
        xt = _ffn(xt, g[4], g[5], w_ffn, l, 1)
    return xt.reshape(batch, seq, d)
```

```python
import jax
import jax.numpy as jnp
from jax import lax
from jax.experimental import pallas as pl
from jax.experimental.pallas import tpu as pltpu

F32 = jnp.float32
BF16 = jnp.bfloat16

HEAD_DIM = 64
N_Q_HEADS = 16
N_KV_HEADS = 4
Q_PER_KV = N_Q_HEADS // N_KV_HEADS
ATTN_WIDTH = N_Q_HEADS * HEAD_DIM
KV_WIDTH = N_KV_HEADS * HEAD_DIM
ATTN_BLOCK = 128
ROPE_THETA = 10000.0
GLA_HEADS = 4
GLA_DK = 128
GLA_DV = 256
GLA_K_WIDTH = GLA_HEADS * GLA_DK
GLA_V_WIDTH = GLA_HEADS * GLA_DV
GLA_GATE_RANK = 16
GLA_TAU = 16.0
GLA_CHUNK = 64
NORM_EPS = 1e-6

LANES = 128
VMEM_LIMIT_BYTES = 60000 * 1024

FFN_TM = 1024
FFN_TF = 512
FFN_LAST_PARTS = 4
PROJ_TM = 512
OUTPROJ_PARTS = 1
ATTN_TQ = 512
GLA_TS = 512
CAST_ROWS = 256
CAST_OUT_STEPS = 4


def _params(*sem):
    return pltpu.CompilerParams(dimension_semantics=sem, vmem_limit_bytes=VMEM_LIMIT_BYTES)


def _resident(shape, lead=()):
    lead = tuple(lead) if isinstance(lead, (tuple, list)) else (lead,)
    return pl.BlockSpec((None,) * len(lead) + tuple(shape), lambda *_: lead + (0,) * len(shape),
                        pipeline_mode=pl.Buffered(1))


def _rms(x, gain, halve=False):
    ms = jnp.mean(x * x, axis=-1, keepdims=True)
    inv = lax.rsqrt(ms + NORM_EPS)
    if halve:
        inv = inv * 0.5
    return (x * inv) * gain


def _silu(x):
    return x * jax.nn.sigmoid(x)


def _swiglu_chunk(hn, wgu, wo):
    width = wo.shape[0]
    gu = jnp.dot(hn, wgu, preferred_element_type=F32)
    act = (_silu(gu[:, :width]) * gu[:, width:]).astype(BF16)
    return jnp.dot(act, wo, preferred_element_type=F32)


def _ffn_kernel(x_ref, gin_ref, gout_ref, wgu_ref, wo_ref, wgut_ref, wot_ref, o_ref, hn_ref):
    j = pl.program_id(1)
    last = pl.num_programs(1) - 1

    @pl.when(j == 0)
    def _():
        hn = _rms(x_ref[...], gin_ref[...]).astype(BF16)
        hn_ref[...] = hn
        o_ref[...] = _swiglu_chunk(hn, wgu_ref[...], wo_ref[...])

    @pl.when((j > 0) & (j < last))
    def _():
        o_ref[...] += _swiglu_chunk(hn_ref[...], wgu_ref[...], wo_ref[...])

    @pl.when(j == last)
    def _():
        part = o_ref.shape[0] // FFN_LAST_PARTS
        for r in range(FFN_LAST_PARTS):
            rows = slice(r * part, (r + 1) * part)
            h = o_ref[rows, :] + _swiglu_chunk(hn_ref[rows, :], wgut_ref[...], wot_ref[...])
            o_ref[rows, :] = x_ref[rows, :] + _rms(h, gout_ref[...], halve=True)


def _ffn(x, gin, gout, w, layer, slot):
    wgu, wo, wgut, wot = w
    t, d = x.shape
    n_full = wgu.shape[2]
    tail = wot.shape[2]
    assert n_full >= 2 and wo.shape[2] == n_full * FFN_TF + tail and 0 < tail and tail % LANES == 0
    main = lambda j: jnp.minimum(j, n_full - 1)
    return pl.pallas_call(
        _ffn_kernel,
        out_shape=jax.ShapeDtypeStruct((t, d), F32),
        grid=(t // FFN_TM, n_full + 1),
        in_specs=[
            pl.BlockSpec((FFN_TM, d), lambda i, j: (i, 0)),
            pl.BlockSpec((1, d), lambda i, j: (0, 0)),
            pl.BlockSpec((1, d), lambda i, j: (0, 0)),
            pl.BlockSpec((None, None, None, d, 2 * FFN_TF),
                         lambda i, j: (layer, slot, main(j), 0, 0)),
            pl.BlockSpec((None, None, FFN_TF, d), lambda i, j: (layer, slot, main(j), 0)),
            _resident((d, 2 * tail), (layer, slot)),
            _resident((tail, d), (layer, slot)),
        ],
        out_specs=pl.BlockSpec((FFN_TM, d), lambda i, j: (i, 0)),
        scratch_shapes=[pltpu.VMEM((FFN_TM, d), BF16)],
        compiler_params=_params("parallel", "arbitrary"),
        name="ffn",
    )(x, gin, gout, wgu, wo, wgut, wot)


def _cast_w_in_kernel(w_ref, main_ref, tail_ref):
    d_ff = w_ref.shape[1] // 2
    for c in range(main_ref.shape[0]):
        for half in range(2):
            src = half * d_ff + c * FFN_TF
            main_ref[c, :, half * FFN_TF:(half + 1) * FFN_TF] = w_ref[:, src:src + FFN_TF].astype(BF16)
    tail = tail_ref.shape[1] // 2
    for half in range(2):
        src = (half + 1) * d_ff - tail
        tail_ref[:, half * tail:(half + 1) * tail] = w_ref[:, src:src + tail].astype(BF16)


def _cast_w_in(ffn_w_in):
    depth, two, d, f2 = ffn_w_in.shape
    n_full = (f2 // 2) // FFN_TF
    tail = f2 // 2 - n_full * FFN_TF
    rt = CAST_ROWS
    pick = lambda m, i: (m // two, m % two)
    return pl.pallas_call(
        _cast_w_in_kernel,
        out_shape=[jax.ShapeDtypeStruct((depth, two, n_full, d, 2 * FFN_TF), BF16),
                   jax.ShapeDtypeStruct((depth, two, d, 2 * tail), BF16)],
        grid=(depth * two, d // rt),
        in_specs=[pl.BlockSpec((None, None, rt, f2), lambda m, i: pick(m, i) + (i, 0))],
        out_specs=[pl.BlockSpec((None, None, n_full, rt, 2 * FFN_TF), lambda m, i: pick(m, i) + (0, i, 0)),
                   pl.BlockSpec((None, None, rt, 2 * tail), lambda m, i: pick(m, i) + (i, 0))],
        compiler_params=_params("parallel", "parallel"),
        name="cast_ffn_w_in",
    )(ffn_w_in)


def _cast_rows_kernel(w_ref, o_ref):
    o_ref[...] = w_ref[...].astype(BF16)


def _cast_w_out(ffn_w_out):
    depth, two, f, d = ffn_w_out.shape
    rt = f // CAST_OUT_STEPS
    assert rt * CAST_OUT_STEPS == f and rt % 16 == 0
    spec = pl.BlockSpec((None, None, rt, d), lambda m, i: (m // two, m % two, i, 0))
    return pl.pallas_call(
        _cast_rows_kernel,
        out_shape=jax.ShapeDtypeStruct(ffn_w_out.shape, BF16),
        grid=(depth * two, CAST_OUT_STEPS),
        in_specs=[spec],
        out_specs=spec,
        compiler_params=_params("parallel", "parallel"),
        name="cast_ffn_w_out",
    )(ffn_w_out)


def _rope_table_kernel(pos_ref, invf_ref, cos_ref, sin_ref):
    ang = pos_ref[...].astype(F32) * invf_ref[...]
    lane = lax.broadcasted_iota(jnp.int32, ang.shape, 1)
    first_half = (lane % HEAD_DIM) < (HEAD_DIM // 2)
    cos_ref[...] = jnp.cos(ang)
    sin_ref[...] = jnp.where(first_half, -jnp.sin(ang), jnp.sin(ang))


def _rope_tables(positions):
    t = positions.size
    tm = min(t, 2048)
    inv_freq = ROPE_THETA ** (-jnp.arange(0, HEAD_DIM, 2, dtype=F32) / HEAD_DIM)
    invf = jnp.tile(inv_freq, LANES // (HEAD_DIM // 2)).reshape(1, LANES)
    return pl.pallas_call(
        _rope_table_kernel,
        out_shape=[jax.ShapeDtypeStruct((t, LANES), F32)] * 2,
        grid=(t // tm,),
        in_specs=[pl.BlockSpec((tm, 1), lambda i: (i, 0)),
                  pl.BlockSpec((1, LANES), lambda i: (0, 0))],
        out_specs=[pl.BlockSpec((tm, LANES), lambda i: (i, 0))] * 2,
        compiler_params=_params("parallel"),
        name="rope_tables",
    )(positions.reshape(t, 1), invf)


def _rope(x, cos, sin, first_half):
    half = HEAD_DIM // 2
    partner = jnp.where(first_half, pltpu.roll(x, LANES - half, axis=1), pltpu.roll(x, half, axis=1))
    return x * cos + partner * sin


def _inproj_kernel(x_ref, g_ref, w_ref, w2_ref, gb_ref, cos_ref, sin_ref,
                   qa_ref, ka_ref, va_ref, qg_ref, kg_ref, vg_ref, rg_ref, la_ref):
    hn = _rms(x_ref[...], g_ref[...]).astype(BF16)
    cos = cos_ref[...]
    sin = sin_ref[...]
    lane = lax.broadcasted_iota(jnp.int32, cos.shape, 1)
    first_half = (lane % HEAD_DIM) < (HEAD_DIM // 2)

    def proj(c0, width):
        return jnp.dot(hn, w_ref[:, c0:c0 + width], preferred_element_type=F32)

    def dup_into(dst_ref, s, slab):
        lo = lane < HEAD_DIM
        swapped = pltpu.roll(slab, HEAD_DIM, axis=1)
        dst_ref[:, (2 * s) * LANES:(2 * s + 1) * LANES] = jnp.where(lo, slab, swapped).astype(BF16)
        dst_ref[:, (2 * s + 1) * LANES:(2 * s + 2) * LANES] = jnp.where(lo, swapped, slab).astype(BF16)

    c_gate = ATTN_WIDTH + 2 * KV_WIDTH + 2 * GLA_K_WIDTH + 2 * GLA_V_WIDTH
    g_lr = proj(c_gate, w_ref.shape[1] - c_gate).astype(BF16)
    z = jnp.dot(g_lr, w2_ref[...], preferred_element_type=F32) + gb_ref[...]
    la_ref[...] = jax.nn.log_sigmoid(z) / GLA_TAU

    c = 0
    qa = proj(c, ATTN_WIDTH)
    for s in range(ATTN_WIDTH // LANES):
        cols = slice(s * LANES, (s + 1) * LANES)
        qa_ref[:, cols] = (_rope(qa[:, cols], cos, sin, first_half) * HEAD_DIM ** -0.5).astype(BF16)
    c += ATTN_WIDTH
    kv = proj(c, 2 * KV_WIDTH)
    for s in range(KV_WIDTH // LANES):
        dup_into(ka_ref, s, _rope(kv[:, s * LANES:(s + 1) * LANES], cos, sin, first_half))
        dup_into(va_ref, s, kv[:, KV_WIDTH + s * LANES:KV_WIDTH + (s + 1) * LANES])
    c += 2 * KV_WIDTH
    qg_ref[...] = proj(c, GLA_K_WIDTH)
    c += GLA_K_WIDTH
    kg_ref[...] = proj(c, GLA_K_WIDTH)
    c += GLA_K_WIDTH
    vg_ref[...] = proj(c, GLA_V_WIDTH).astype(BF16)
    c += GLA_V_WIDTH
    rg_ref[...] = proj(c, GLA_V_WIDTH).astype(BF16)
    assert c + GLA_V_WIDTH == c_gate


def _inproj(x, gain, w_mix, w2, gate_b, cos, sin, layer):
    t, d = x.shape
    tm = PROJ_TM
    row = lambda width: pl.BlockSpec((tm, width), lambda i: (i, 0))
    widths = (ATTN_WIDTH, 2 * KV_WIDTH, 2 * KV_WIDTH, GLA_K_WIDTH, GLA_K_WIDTH, GLA_V_WIDTH,
              GLA_V_WIDTH, GLA_K_WIDTH)
    dtypes = (BF16, BF16, BF16, F32, F32, BF16, BF16, F32)
    return pl.pallas_call(
        _inproj_kernel,
        out_shape=[jax.ShapeDtypeStruct((t, w), dt) for w, dt in zip(widths, dtypes)],
        grid=(t // tm,),
        in_specs=[row(d), _resident((1, d)), _resident(w_mix.shape[1:], layer),
                  _resident(w2.shape[1:], layer), _resident((1, GLA_K_WIDTH)),
                  row(LANES), row(LANES)],
        out_specs=[row(w) for w in widths],
        compiler_params=_params("parallel"),
        name="mixer_inproj",
    )(x, gain, w_mix, w2, gate_b, cos, sin)


def _attn_kernel(sinks_ref, q_ref, kc_ref, vc_ref, kp_ref, vp_ref, o_ref):
    n = pl.program_id(1)
    blk = ATTN_BLOCK
    k_all = jnp.concatenate([kp_ref[...], kc_ref[...]], axis=0)
    v_all = jnp.concatenate([vp_ref[...], vc_ref[...]], axis=0)
    qi = lax.broadcasted_iota(jnp.int32, (2 * blk, blk), 0) % blk
    kj = lax.broadcasted_iota(jnp.int32, (2 * blk, blk), 1)
    top = lax.broadcasted_iota(jnp.int32, (2 * blk, 1), 0) < blk
    band_prev = kj > qi
    band_cur = kj <= qi
    band_prev_first = band_prev & (kj >= jnp.where(n > 0, 0, blk))
    lane = lax.broadcasted_iota(jnp.int32, (2 * blk, LANES), 1)
    key = lax.broadcasted_iota(jnp.int32, (2 * blk, LANES), 0)
    half = (lane < HEAD_DIM, lane >= HEAD_DIM)
    ones_half = tuple(jnp.where(hm, 1.0, 0.0).astype(BF16) for hm in half)
    zero = jnp.zeros((2 * blk, LANES), BF16)
    neg_inf = jnp.full((2 * blk, blk), -jnp.inf, F32)
    for g in range(N_KV_HEADS):
        gl = slice(g * LANES, (g + 1) * LANES)
        fills = []
        for parity in range(2):
            sink = jnp.where(top, sinks_ref[Q_PER_KV * g + parity], sinks_ref[Q_PER_KV * g + 2 + parity])
            fills.append(jnp.where(kj == 0, sink, neg_inf))
        for b in range(q_ref.shape[0] // blk):
            rows = slice(b * blk, (b + 1) * blk)
            q = jnp.concatenate([q_ref[rows, (2 * g) * LANES:(2 * g + 1) * LANES],
                                 q_ref[rows, (2 * g + 1) * LANES:(2 * g + 2) * LANES]], axis=0)
            kd = k_all[b * blk:(b + 2) * blk, gl]
            vd = v_all[b * blk:(b + 2) * blk, gl]
            mask_prev = band_prev_first if b == 0 else band_prev
            acc = None
            for parity in range(2):
                kz = jnp.where(half[parity], kd, zero)
                s = lax.dot_general(q, kz, (((1,), (1,)), ((), ())), preferred_element_type=F32)
                s_prev = jnp.where(mask_prev, s[:, :blk], fills[parity])
                s_cur = jnp.where(band_cur, s[:, blk:], neg_inf)
                m = jnp.max(jnp.maximum(s_prev, s_cur), axis=-1, keepdims=True)
                p = jnp.concatenate([jnp.exp(s_prev - m), jnp.exp(s_cur - m)], axis=1).astype(BF16)
                vz = jnp.where(half[parity] & (key > 0), vd, zero)
                w = jnp.concatenate([vz, ones_half[parity]], axis=1)
                part = jnp.dot(p, w, preferred_element_type=F32)
                acc = part if acc is None else acc + part
            out = (acc[:, :LANES] / acc[:, LANES:]).astype(BF16)
            o_ref[rows, (2 * g) * LANES:(2 * g + 1) * LANES] = out[:blk]
            o_ref[rows, (2 * g + 1) * LANES:(2 * g + 2) * LANES] = out[blk:]


def _attention(q, k, v, sinks, batch):
    t = q.shape[0]
    seq = t // batch
    tq = min(ATTN_TQ, seq)
    nq = seq // tq
    per = tq // ATTN_BLOCK
    kvw = k.shape[1]
    cur = lambda width: pl.BlockSpec((tq, width), lambda b, n: (b * nq + n, 0))
    prev = lambda width: pl.BlockSpec(
        (ATTN_BLOCK, width), lambda b, n: (jnp.maximum((b * nq + n) * per - 1, 0), 0))
    return pl.pallas_call(
        _attn_kernel,
        out_shape=jax.ShapeDtypeStruct((t, ATTN_WIDTH), BF16),
        grid=(batch, nq),
        in_specs=[pl.BlockSpec(memory_space=pltpu.SMEM),
                  cur(ATTN_WIDTH), cur(kvw), cur(kvw), prev(kvw), prev(kvw)],
        out_specs=cur(ATTN_WIDTH),
        compiler_params=_params("parallel", "arbitrary"),
        name="swa_attention",
    )(sinks, q, k, v, k, v)


def _gla_kernel(q_ref, k_ref, v_ref, la_ref, r_ref, gain_ref, o_ref, state_ref):
    @pl.when(pl.program_id(1) == 0)
    def _():
        state_ref[...] = jnp.zeros_like(state_ref)

    c = GLA_CHUNK
    pc = 2 * c
    ti = lax.broadcasted_iota(jnp.int32, (pc, pc), 0)
    si = lax.broadcasted_iota(jnp.int32, (pc, pc), 1)
    same_chunk = (ti >= c) == (si >= c)
    causal = same_chunk & (ti >= si)
    cross = (ti >= c) & (si < c)
    tri = causal.astype(F32)
    second = lax.broadcasted_iota(jnp.int32, (pc, 1), 0) >= c
    gain = gain_ref[...]
    scale = GLA_DK ** -0.5
    nt = (((1,), (1,)), ((), ()))
    state_t = [state_ref[h] for h in range(GLA_HEADS)]
    for pi in range(q_ref.shape[0] // pc):
        rows = slice(pi * pc, (pi + 1) * pc)
        la = la_ref[rows, :]
        b = jnp.dot(tri, la, precision=lax.Precision.HIGHEST, preferred_element_type=F32)
        b_last0 = b[c - 1:c, :]
        b_last1 = b[pc - 1:pc, :]
        decay0 = jnp.exp(b_last0)
        decay1 = jnp.exp(b_last1)
        q_dec = (q_ref[rows, :] * scale) * jnp.exp(b)
        k_all = k_ref[rows, :]
        k_dec = k_all * jnp.exp(-b)
        k_rem = k_all * jnp.exp(jnp.where(second, b_last1, b_last0) - b)
        q_in = jnp.where(second, q_dec * decay0, q_dec)
        k_out = jnp.where(second, k_rem, k_rem * decay1)
        for h in range(GLA_HEADS):
            kl = slice(h * GLA_DK, (h + 1) * GLA_DK)
            vl = slice(h * GLA_DV, (h + 1) * GLA_DV)
            qh = q_dec[:, kl].astype(BF16)
            qih = q_in[:, kl].astype(BF16)
            v_t = v_ref[rows, vl].astype(F32).T.astype(BF16)
            s2 = lax.dot_general(jnp.concatenate([qh, qih], axis=0), k_dec[:, kl].astype(BF16), nt,
                                 preferred_element_type=F32)
            intra = (jnp.where(causal, s2[:pc], 0.0) + jnp.where(cross, s2[pc:], 0.0)).astype(BF16)
            o = lax.dot_general(jnp.concatenate([qih, intra], axis=1),
                                jnp.concatenate([state_t[h].astype(BF16), v_t], axis=1),
                                nt, preferred_element_type=F32)
            d_state_t = jnp.dot(v_t, k_out[:, kl].astype(BF16), preferred_element_type=F32)
            state_t[h] = state_t[h] * (decay0[:, kl] * decay1[:, kl]) + d_state_t
            r = r_ref[rows, vl].astype(F32)
            o_ref[rows, vl] = (_rms(o, gain) * _silu(r)).astype(BF16)
    for h in range(GLA_HEADS):
        state_ref[h] = state_t[h]


def _gla(q, k, v, log_a, r, gain, batch):
    t = q.shape[0]
    seq = t // batch
    ts = min(GLA_TS, seq)
    ns = seq // ts
    row = lambda width: pl.BlockSpec((ts, width), lambda b, n: (b * ns + n, 0))
    return pl.pallas_call(
        _gla_kernel,
        out_shape=jax.ShapeDtypeStruct((t, GLA_V_WIDTH), BF16),
        grid=(batch, ns),
        in_specs=[row(GLA_K_WIDTH), row(GLA_K_WIDTH), row(GLA_V_WIDTH), row(GLA_K_WIDTH),
                  row(GLA_V_WIDTH), pl.BlockSpec((1, GLA_DV), lambda b, n: (0, 0))],
        out_specs=row(GLA_V_WIDTH),
        scratch_shapes=[pltpu.VMEM((GLA_HEADS, GLA_DV, GLA_DK), F32)],
        compiler_params=_params("parallel", "arbitrary"),
        name="gla",
    )(q, k, v, log_a, r, gain)


def _mixers_kernel(sinks_ref, qa_ref, kc_ref, vc_ref, kp_ref, vp_ref,
                   qg_ref, kg_ref, vg_ref, la_ref, rg_ref, gain_ref, ao_ref, go_ref, state_ref):
    _gla_kernel(qg_ref, kg_ref, vg_ref, la_ref, rg_ref, gain_ref, go_ref, state_ref)
    _attn_kernel(sinks_ref, qa_ref, kc_ref, vc_ref, kp_ref, vp_ref, ao_ref)


def _mixers(qa, ka, va, sinks, qg, kg, vg, log_a, rg, gain, batch):
    t = qa.shape[0]
    seq = t // batch
    ts = min(ATTN_TQ, seq)
    assert ts == min(GLA_TS, seq)
    ns = seq // ts
    per = ts // ATTN_BLOCK
    kvw = ka.shape[1]
    row = lambda width: pl.BlockSpec((ts, width), lambda b, n: (b * ns + n, 0))
    prev = lambda width: pl.BlockSpec(
        (ATTN_BLOCK, width), lambda b, n: (jnp.maximum((b * ns + n) * per - 1, 0), 0))
    return pl.pallas_call(
        _mixers_kernel,
        out_shape=[jax.ShapeDtypeStruct((t, ATTN_WIDTH), BF16),
                   jax.ShapeDtypeStruct((t, GLA_V_WIDTH), BF16)],
        grid=(batch, ns),
        in_specs=[pl.BlockSpec(memory_space=pltpu.SMEM),
                  row(ATTN_WIDTH), row(kvw), row(kvw), prev(kvw), prev(kvw),
                  row(GLA_K_WIDTH), row(GLA_K_WIDTH), row(GLA_V_WIDTH), row(GLA_K_WIDTH),
                  row(GLA_V_WIDTH), pl.BlockSpec((1, GLA_DV), lambda b, n: (0, 0))],
        out_specs=[row(ATTN_WIDTH), row(GLA_V_WIDTH)],
        scratch_shapes=[pltpu.VMEM((GLA_HEADS, GLA_DV, GLA_DK), F32)],
        compiler_params=_params("parallel", "arbitrary"),
        name="token_mixers",
    )(sinks, qa, ka, va, ka, va, qg, kg, vg, log_a, rg, gain)


def _outproj_kernel(x_ref, a_ref, g_ref, w_ref, gain_ref, o_ref):
    part = o_ref.shape[0] // OUTPROJ_PARTS
    for r in range(OUTPROJ_PARTS):
        rows = slice(r * part, (r + 1) * part)
        h = (jnp.dot(a_ref[rows, :], w_ref[:ATTN_WIDTH, :], preferred_element_type=F32)
             + jnp.dot(g_ref[rows, :], w_ref[ATTN_WIDTH:, :], preferred_element_type=F32))
        o_ref[rows, :] = x_ref[rows, :] + _rms(h, gain_ref[...])


def _outproj(x, attn, gla, w_out, gain, layer):
    t, d = x.shape
    tm = PROJ_TM
    row = lambda width: pl.BlockSpec((tm, width), lambda i: (i, 0))
    return pl.pallas_call(
        _outproj_kernel,
        out_shape=jax.ShapeDtypeStruct((t, d), F32),
        grid=(t // tm,),
        in_specs=[row(d), row(ATTN_WIDTH), row(GLA_V_WIDTH), _resident(w_out.shape[1:], layer),
                  _resident((1, d))],
        out_specs=row(d),
        compiler_params=_params("parallel"),
        name="mixer_outproj",
    )(x, attn, gla, w_out, gain)


def _prep_ffn_weights(ffn_w_in, ffn_w_out):
    full = (ffn_w_out.shape[-2] // FFN_TF) * FFN_TF
    wgu, wgut = _cast_w_in(ffn_w_in)
    return wgu, _cast_w_out(ffn_w_out), wgut, ffn_w_out[..., full:, :].astype(BF16)


def _prep_mixer_weights(w_mix_in, gla_gate_w2, w_mix_out):
    return w_mix_in.astype(BF16), gla_gate_w2.astype(BF16), w_mix_out.astype(BF16)


def kernel(x, positions, norm_gains, ffn_w_in, ffn_w_out, w_mix_in, attn_sinks, gla_gate_w2,
           gla_gate_b, gla_norm_gain, w_mix_out):
    batch, seq, d = x.shape
    depth = norm_gains.shape[0]
    xt = x.reshape(batch * seq, d)
    cos, sin = _rope_tables(positions)
    w_ffn = _prep_ffn_weights(ffn_w_in, ffn_w_out)
    w_mix, w2, w_mo = _prep_mixer_weights(w_mix_in, gla_gate_w2, w_mix_out)
    for l in range(depth):
        g = norm_gains[l].reshape(-1, 1, d)
        xt = _ffn(xt, g[0], g[1], w_ffn, l, 0)
        qa, ka, va, qg, kg, vg, rg, la = _inproj(
            xt, g[2], w_mix, w2, gla_gate_b[l].reshape(1, -1), cos, sin, l)
        attn, gla = _mixers(qa, ka, va, attn_sinks[l], qg, kg, vg, la, rg,
                            gla_norm_gain[l].reshape(1, -1), batch)
        xt = _outproj(xt, attn, gla, w_mo, g[3], l)
        xt = _ffn(xt, g[4], g[5], w_ffn, l, 1)
    return xt.reshape(batch, seq, d)
```
